```python
import math
import jax, jax.numpy as jnp
from jax import lax
import numpy as np

D_MODEL = 1024
BATCH = 16
SEQ = 4096
DEPTH = 1
DEC_BATCH = 128
DEC_SEQ = 8
PAST_LEN = 8192
PAGE_SIZE = 128

NSA_HEADS = 8
NSA_KV_HEADS = 2
HEAD_DIM = 64
GROUP = NSA_HEADS // NSA_KV_HEADS
NSA_WIDTH = NSA_HEADS * HEAD_DIM
CMP_BLOCK = 32
CMP_STRIDE = 16
SEL_BLOCK = 64
SEL_TOP = 16
WINDOW = 512
Q_BLOCK = 64
SEL_FORCE = 1e9
NEG_INF = -1e30

RW_HEADS = 8
RW_HEAD_DIM = 64
RW_WIDTH = RW_HEADS * RW_HEAD_DIM
RW_DECAY_LORA = 64
RW_AAA_LORA = 64
RW_GATE_LORA = 128
RW_COLS = 3 * RW_WIDTH + RW_DECAY_LORA + RW_AAA_LORA + RW_GATE_LORA
GN_EPS = 64e-5

N_EXPERTS = 32
TOP_K = 4
D_FF = 1024
SWIGLU_LIMIT = 7.0
SWIGLU_ALPHA = 1.702
MOE_BLOCK = 128
RMS_EPS = 1e-5

OFF_KV = NSA_WIDTH
OFF_GN = OFF_KV + 6 * NSA_KV_HEADS * HEAD_DIM
OFF_RW = OFF_GN + 3 * NSA_HEADS
OFF_MG = OFF_RW + RW_COLS
N_PROJ = OFF_MG + 2 * D_MODEL

kernel_name = 'nsa_rwkv7_gated_hybrid_moe_step'


def rmsnorm(x, g):
    xf = x.astype(jnp.float32)
    y = xf * lax.rsqrt(jnp.mean(xf * xf, axis=-1, keepdims=True) + RMS_EPS) * g.astype(jnp.float32)
    return y.astype(x.dtype)


def masked_softmax(s, mask):
    s = jnp.where(mask, s, NEG_INF)
    m = jnp.max(s, axis=-1, keepdims=True)
    e = jnp.where(mask, jnp.exp(s - m), 0.0)
    return e / jnp.maximum(jnp.sum(e, axis=-1, keepdims=True), 1e-30)


def compress(rows, pe, mix, w1, w2):
    B, L = rows.shape[:2]
    n_sub = CMP_BLOCK // CMP_STRIDE
    n_chunk = L // CMP_STRIDE
    nc = n_chunk - n_sub + 1
    ch = rows.reshape(B, n_chunk, CMP_STRIDE, NSA_KV_HEADS, HEAD_DIM).astype(jnp.float32)
    mix = mix.astype(jnp.float32)
    c = jnp.sum(mix * pe.astype(jnp.float32), axis=0)
    for j in range(n_sub):
        part = jnp.einsum('bcpgd,pd->bcgd', ch, mix[j * CMP_STRIDE:(j + 1) * CMP_STRIDE])
        c = c + part[:, j:j + nc]
    return jax.nn.gelu(c @ w1.astype(jnp.float32)) @ w2.astype(jnp.float32)


def nsa_attention(q, q_start, kv_full, kw_full, kw_start, cmp_pe, cmp_mix, cmp_w1, cmp_w2):
    B, Sq = q.shape[:2]
    Lp = kv_full.shape[1]
    kc = compress(kv_full[:, :, 0], cmp_pe[0], cmp_mix[0], cmp_w1[0], cmp_w2[0])
    vc = compress(kv_full[:, :, 1], cmp_pe[1], cmp_mix[1], cmp_w1[1], cmp_w2[1])
    nc = kc.shape[1]
    ns = Lp // SEL_BLOCK
    n_top = min(SEL_TOP, ns)
    cmp_end = jnp.arange(nc) * CMP_STRIDE + CMP_BLOCK - 1
    ci = jnp.arange(nc)[:, None] * CMP_STRIDE
    sj = jnp.arange(ns)[None, :] * SEL_BLOCK
    cover = ((ci < sj + SEL_BLOCK) & (ci + CMP_BLOCK > sj)).astype(jnp.float32)
    ksb = kv_full[:, :, 2].reshape(B, ns, SEL_BLOCK, NSA_KV_HEADS, HEAD_DIM).transpose(0, 3, 1, 2, 4)
    vsb = kv_full[:, :, 3].reshape(B, ns, SEL_BLOCK, NSA_KV_HEADS, HEAD_DIM).transpose(0, 3, 1, 2, 4)
    kw_pad = jnp.pad(kw_full, ((0, 0), (WINDOW, 0), (0, 0), (0, 0), (0, 0)))
    qb = math.gcd(Sq, Q_BLOCK)
    nqb = Sq // qb
    scale = HEAD_DIM ** -0.5
    g_ix = jnp.arange(NSA_KV_HEADS)[None, :, None]

    def one(i):
        b = i // nqb
        s0 = (i % nqb) * qb
        t = q_start + s0 + jnp.arange(qb)
        qg = lax.dynamic_slice_in_dim(q[b], s0, qb, 0).astype(jnp.float32)
        qg = qg.reshape(qb, NSA_KV_HEADS, GROUP, HEAD_DIM)
        s_c = jnp.einsum('qgnd,cgd->qgnc', qg, kc[b]) * scale
        p_c = masked_softmax(s_c, (cmp_end[None, :] <= t[:, None])[:, None, None, :])
        o_c = jnp.einsum('qgnc,cgd->qgnd', p_c, vc[b])
        imp = jnp.einsum('qgnc,cs->qgs', p_c, cover)
        cur = (t // SEL_BLOCK)[:, None, None]
        jb = jnp.arange(ns)[None, None, :]
        forced = (jb == 0) | (jb == cur) | (jb == cur - 1)
        score = jnp.where(jb <= cur, jnp.where(forced, SEL_FORCE, imp), -SEL_FORCE)
        _, idx = lax.top_k(score, n_top)
        ksg = ksb[b][g_ix, idx].astype(jnp.float32).reshape(qb, NSA_KV_HEADS, n_top * SEL_BLOCK, HEAD_DIM)
        vsg = vsb[b][g_ix, idx].astype(jnp.float32).reshape(qb, NSA_KV_HEADS, n_top * SEL_BLOCK, HEAD_DIM)
        kpos = (idx[..., None] * SEL_BLOCK + jnp.arange(SEL_BLOCK)).reshape(qb, NSA_KV_HEADS, n_top * SEL_BLOCK)
        s_s = jnp.einsum('qgnd,qgkd->qgnk', qg, ksg) * scale
        p_s = masked_softmax(s_s, (kpos <= t[:, None, None])[:, :, None, :])
        o_s = jnp.einsum('qgnk,qgkd->qgnd', p_s, vsg)
        kwb = lax.dynamic_slice_in_dim(kw_pad[b], q_start - kw_start + s0, WINDOW + qb, 0).astype(jnp.float32)
        kpos_w = q_start + s0 - WINDOW + jnp.arange(WINDOW + qb)
        dist = t[:, None] - kpos_w[None, :]
        wmask = (kpos_w >= kw_start)[None, :] & (dist >= 0) & (dist <= WINDOW)
        s_w = jnp.einsum('qgnd,kgd->qgnk', qg, kwb[:, 0]) * scale
        p_w = masked_softmax(s_w, wmask[:, None, None, :])
        o_w = jnp.einsum('qgnk,kgd->qgnd', p_w, kwb[:, 1])
        return (o_c.reshape(qb, NSA_HEADS, HEAD_DIM), o_s.reshape(qb, NSA_HEADS, HEAD_DIM),
                o_w.reshape(qb, NSA_HEADS, HEAD_DIM))

    o_c, o_s, o_w = lax.map(one, jnp.arange(B * nqb))
    shp = (B, Sq, NSA_HEADS, HEAD_DIM)
    return o_c.reshape(shp), o_s.reshape(shp), o_w.reshape(shp)


def rwkv7_mix(cols, prev_row, state0, lp):
    Bn, Sn, _ = cols.shape
    cols = cols.astype(jnp.float32)
    prev = jnp.concatenate([prev_row[:, None].astype(jnp.float32), cols[:, :-1]], axis=1)
    xs = cols + lp['rw_mu'].astype(jnp.float32) * (prev - cols)
    o1, o2, o3 = RW_WIDTH, 2 * RW_WIDTH, 3 * RW_WIDTH
    o4 = o3 + RW_DECAY_LORA
    o5 = o4 + RW_AAA_LORA
    r, k, v = xs[..., :o1], xs[..., o1:o2], xs[..., o2:o3]
    wl, al, gl = xs[..., o3:o4], xs[..., o4:o5], xs[..., o5:]
    w = -jax.nn.softplus(-(lp['rw_w0'] + jnp.tanh(wl) @ lp['rw_w2'])) - 0.5
    decay = jnp.exp(-jnp.exp(w))
    a = jax.nn.sigmoid(lp['rw_a0'] + al @ lp['rw_a2'])
    g = jax.nn.sigmoid(gl) @ lp['rw_g2']
    hv = lambda z: z.reshape(Bn, Sn, RW_HEADS, RW_HEAD_DIM)
    kk = hv(k * lp['rw_kk'])
    kk = kk / jnp.maximum(jnp.sqrt(jnp.sum(kk * kk, axis=-1, keepdims=True)), 1e-12)
    k = k * (1.0 + (a - 1.0) * lp['rw_ka'])
    r, k, v, a, decay = hv(r), hv(k), hv(v), hv(a), hv(decay)
    seq = tuple(jnp.swapaxes(z, 0, 1).astype(jnp.float32) for z in (r, decay, k, v, kk, a))

    def step(st, inp):
        rt, dt, kt, vt, kkt, at = inp
        sa = jnp.einsum('bhvk,bhk->bhv', st, -kkt)
        st = st * dt[:, :, None, :] + sa[..., None] * (kkt * at)[:, :, None, :] + vt[..., None] * kt[:, :, None, :]
        return st, jnp.einsum('bhvk,bhk->bhv', st, rt)

    st_fin, y = lax.scan(step, state0.astype(jnp.float32), seq)
    y = jnp.swapaxes(y, 0, 1)
    mu = jnp.mean(y, axis=-1, keepdims=True)
    var = jnp.mean(jnp.square(y - mu), axis=-1, keepdims=True)
    y = ((y - mu) * lax.rsqrt(var + GN_EPS)).reshape(Bn, Sn, RW_WIDTH) * lp['rw_ln_w'] + lp['rw_ln_b']
    bonus = jnp.sum(r * k * lp['rw_rk'].reshape(RW_HEADS, RW_HEAD_DIM), axis=-1, keepdims=True) * v
    y = (y + bonus.reshape(Bn, Sn, RW_WIDTH)) * g
    return y, st_fin, cols[:, -1]


def moe_ffn(x, lp):
    T = x.shape[0]
    logits = x.astype(jnp.float32) @ lp['w_router'].astype(jnp.float32) + lp['b_router'].astype(jnp.float32)
    top_v, top_e = lax.top_k(logits, TOP_K)
    gates = jax.nn.softmax(top_v, axis=-1)
    A = T * TOP_K
    e_flat = top_e.reshape(A)
    order = jnp.argsort(e_flat)
    e_sorted = e_flat[order]
    counts = jnp.bincount(e_flat, length=N_EXPERTS)
    starts = jnp.cumsum(counts) - counts
    padded = (counts + MOE_BLOCK - 1) // MOE_BLOCK * MOE_BLOCK
    pends = jnp.cumsum(padded)
    pstarts = pends - padded
    slot_sorted = pstarts[e_sorted] + jnp.arange(A) - starts[e_sorted]
    n_blocks = -(-A // MOE_BLOCK) + N_EXPERTS
    P = n_blocks * MOE_BLOCK
    slot_tok = jnp.full((P,), T, jnp.int32).at[slot_sorted].set((order // TOP_K).astype(jnp.int32))
    slot_gate = jnp.zeros((P,), jnp.float32).at[slot_sorted].set(gates.reshape(A)[order])
    blk_exp = jnp.minimum(jnp.sum(jnp.arange(n_blocks)[:, None] * MOE_BLOCK >= pends[None, :], axis=1), N_EXPERTS - 1)
    x_pad = jnp.concatenate([x, jnp.zeros((1, D_MODEL), x.dtype)], axis=0)
    xb = x_pad[slot_tok].reshape(n_blocks, MOE_BLOCK, D_MODEL)
    gb = slot_gate.reshape(n_blocks, MOE_BLOCK, 1)
    w_up, b_up, w_down, b_down = lp['w_up'], lp['b_up'], lp['w_down'], lp['b_down']

    def expert_block(args):
        xblk, gblk, e = args
        hu = xblk @ w_up[e] + b_up[e]
        gt = jnp.minimum(hu[:, :D_FF], SWIGLU_LIMIT)
        up = jnp.clip(hu[:, D_FF:], -SWIGLU_LIMIT, SWIGLU_LIMIT)
        hh = (up + 1.0) * gt * jax.nn.sigmoid(SWIGLU_ALPHA * gt)
        return (hh @ w_down[e] + b_down[e]).astype(jnp.float32) * gblk

    yb = lax.map(expert_block, (xb, gb, blk_exp)).reshape(P, D_MODEL)
    y = jnp.zeros((T + 1, D_MODEL), jnp.float32).at[slot_tok].add(yb)
    return y[:T]


def hybrid_layer(x, q_start, kv_past, win_past, wkv0, shift0, n_keep, lp):
    B, Sq, _ = x.shape
    xn = rmsnorm(x, lp['norm1'])
    proj = xn @ lp['w_in']
    q = proj[..., :OFF_KV].reshape(B, Sq, NSA_HEADS, HEAD_DIM)
    kv6 = proj[..., OFF_KV:OFF_GN].reshape(B, Sq, 6, NSA_KV_HEADS, HEAD_DIM)
    g_nsa = jax.nn.sigmoid(proj[..., OFF_GN:OFF_RW].astype(jnp.float32)).reshape(B, Sq, 3, NSA_HEADS, 1)
    rw_cols = proj[..., OFF_RW:OFF_MG]
    g_mg = jax.nn.sigmoid(proj[..., OFF_MG:].astype(jnp.float32)).reshape(B, Sq, 2, D_MODEL)
    kv_new = kv6[:, :, :4].astype(kv_past.dtype)
    win_new = kv6[:, :, 4:].astype(win_past.dtype)
    L = q_start + Sq
    Lp = -(-L // SEL_BLOCK) * SEL_BLOCK
    kv_full = jnp.concatenate([kv_past, kv_new, jnp.zeros((B, Lp - L) + kv_new.shape[2:], kv_new.dtype)], axis=1)
    kw_full = jnp.concatenate([win_past, win_new], axis=1)
    kw_start = q_start - win_past.shape[1]
    o_c, o_s, o_w = nsa_attention(q, q_start, kv_full, kw_full, kw_start,
                                  lp['cmp_pe'], lp['cmp_mix'], lp['cmp_w1'], lp['cmp_w2'])
    o_nsa = (g_nsa[:, :, 0] * o_c + g_nsa[:, :, 1] * o_s + g_nsa[:, :, 2] * o_w).reshape(B, Sq, NSA_WIDTH)
    o_rw, wkv_new, shift_new = rwkv7_mix(rw_cols, shift0, wkv0, lp)
    merged = (g_mg[:, :, 0] * (o_nsa.astype(x.dtype) @ lp['w_pa'])
              + g_mg[:, :, 1] * (o_rw.astype(x.dtype) @ lp['w_pb']))
    h = x + merged.astype(x.dtype) @ lp['w_o']
    f = moe_ffn(rmsnorm(h, lp['norm2']).reshape(B * Sq, D_MODEL), lp).reshape(B, Sq, D_MODEL)
    y = h + f.astype(x.dtype)
    return y, kv_new, kw_full[:, kw_full.shape[1] - n_keep:], wkv_new, shift_new


def setup_inputs(seed: int = 0) -> dict:
    key = jax.random.key(seed)
    ks = iter(jax.random.split(key, 48))
    f32 = jnp.float32

    def nrm(shape, scale):
        return jax.random.normal(next(ks), shape, f32) * scale

    n_pages = PAST_LEN // PAGE_SIZE
    n_phys = (DEC_BATCH * n_pages * 5) // 4
    n_buf = min(WINDOW, PAST_LEN)
    x_prompt = nrm((BATCH, SEQ, D_MODEL), 1.0)
    x_sample = nrm((DEC_BATCH, DEC_SEQ, D_MODEL), 1.0)
    cache_kv = nrm((DEPTH, n_phys, PAGE_SIZE, 4, NSA_KV_HEADS, HEAD_DIM), 1.0)
    cache_win = nrm((DEPTH, DEC_BATCH, n_buf, 2, NSA_KV_HEADS, HEAD_DIM), 1.0)
    state_wkv = nrm((DEPTH, DEC_BATCH, RW_HEADS, RW_HEAD_DIM, RW_HEAD_DIM), 0.3)
    state_shift = nrm((DEPTH, DEC_BATCH, RW_COLS), 1.0)
    page_table = jax.random.permutation(next(ks), n_phys)[:DEC_BATCH * n_pages]
    page_table = page_table.reshape(DEC_BATCH, n_pages).astype(jnp.int32)
    norm1 = 1.0 + nrm((DEPTH, D_MODEL), 0.02)
    w_in = nrm((DEPTH, D_MODEL, N_PROJ), D_MODEL ** -0.5)
    cmp_pe = nrm((DEPTH, 2, CMP_BLOCK, HEAD_DIM), 0.1)
    cmp_mix = (1.0 + nrm((DEPTH, 2, CMP_BLOCK, HEAD_DIM), 0.1)) / CMP_BLOCK
    cmp_w1 = nrm((DEPTH, 2, HEAD_DIM, HEAD_DIM), HEAD_DIM ** -0.5)
    cmp_w2 = nrm((DEPTH, 2, HEAD_DIM, HEAD_DIM), HEAD_DIM ** -0.5)
    rw_mu = jax.random.uniform(next(ks), (DEPTH, RW_COLS), f32)
    rw_w0 = jax.random.uniform(next(ks), (DEPTH, RW_WIDTH), f32, minval=-6.0, maxval=-1.0)
    rw_w2 = nrm((DEPTH, RW_DECAY_LORA, RW_WIDTH), 0.1)
    rw_a0 = nrm((DEPTH, RW_WIDTH), 0.1)
    rw_a2 = nrm((DEPTH, RW_AAA_LORA, RW_WIDTH), 0.1)
    rw_g2 = nrm((DEPTH, RW_GATE_LORA, RW_WIDTH), RW_GATE_LORA ** -0.5)
    rw_kk = 1.0 + nrm((DEPTH, RW_WIDTH), 0.1)
    rw_ka = 1.0 + nrm((DEPTH, RW_WIDTH), 0.1)
    rw_rk = nrm((DEPTH, RW_WIDTH), 0.1)
    rw_ln_w = 1.0 + nrm((DEPTH, RW_WIDTH), 0.02)
    rw_ln_b = nrm((DEPTH, RW_WIDTH), 0.02)
    w_pa = nrm((DEPTH, NSA_WIDTH, D_MODEL), NSA_WIDTH ** -0.5)
    w_pb = nrm((DEPTH, RW_WIDTH, D_MODEL), RW_WIDTH ** -0.5)
    w_o = nrm((DEPTH, D_MODEL, D_MODEL), D_MODEL ** -0.5)
    norm2 = 1.0 + nrm((DEPTH, D_MODEL), 0.02)
    w_router = nrm((DEPTH, D_MODEL, N_EXPERTS), D_MODEL ** -0.5)
    b_router = nrm((DEPTH, N_EXPERTS), 0.01)
    w_up = nrm((DEPTH, N_EXPERTS, D_MODEL, 2 * D_FF), D_MODEL ** -0.5)
    b_up = nrm((DEPTH, N_EXPERTS, 2 * D_FF), 0.01)
    w_down = nrm((DEPTH, N_EXPERTS, D_FF, D_MODEL), D_FF ** -0.5)
    b_down = nrm((DEPTH, N_EXPERTS, D_MODEL), 0.01)
    norm_f = 1.0 + nrm((D_MODEL,), 0.02)
    return {'x_prompt': x_prompt, 'x_sample': x_sample, 'cache_kv': cache_kv, 'cache_win': cache_win,
            'state_wkv': state_wkv, 'state_shift': state_shift, 'page_table': page_table,
            'norm1': norm1, 'w_in': w_in, 'cmp_pe': cmp_pe, 'cmp_mix': cmp_mix, 'cmp_w1': cmp_w1,
            'cmp_w2': cmp_w2, 'rw_mu': rw_mu, 'rw_w0': rw_w0, 'rw_w2': rw_w2, 'rw_a0': rw_a0,
            'rw_a2': rw_a2, 'rw_g2': rw_g2, 'rw_kk': rw_kk, 'rw_ka': rw_ka, 'rw_rk': rw_rk,
            'rw_ln_w': rw_ln_w, 'rw_ln_b': rw_ln_b, 'w_pa': w_pa, 'w_pb': w_pb, 'w_o': w_o,
            'norm2': norm2, 'w_router': w_router, 'b_router': b_router, 'w_up': w_up, 'b_up': b_up,
            'w_down': w_down, 'b_down': b_down, 'norm_f': norm_f}


def reference(x_prompt, x_sample, cache_kv, cache_win, state_wkv, state_shift, page_table,
              norm1, w_in, cmp_pe, cmp_mix, cmp_w1, cmp_w2, rw_mu, rw_w0, rw_w2, rw_a0, rw_a2,
              rw_g2, rw_kk, rw_ka, rw_rk, rw_ln_w, rw_ln_b, w_pa, w_pb, w_o, norm2, w_router,
              b_router, w_up, b_up, w_down, b_down, norm_f):
    B, S, _ = x_prompt.shape
    DB = x_sample.shape[0]
    n_pages = page_table.shape[1]
    past = n_pages * cache_kv.shape[2]
    nbuf = cache_win.shape[2]
    hp, hs = x_prompt, x_sample
    kv_p, kv_s, win_p, win_s, wkv_p, wkv_s, sh_p, sh_s = [], [], [], [], [], [], [], []
    for l in range(DEPTH):
        lp = {'norm1': norm1[l], 'w_in': w_in[l], 'cmp_pe': cmp_pe[l], 'cmp_mix': cmp_mix[l],
              'cmp_w1': cmp_w1[l], 'cmp_w2': cmp_w2[l], 'rw_mu': rw_mu[l], 'rw_w0': rw_w0[l],
              'rw_w2': rw_w2[l], 'rw_a0': rw_a0[l], 'rw_a2': rw_a2[l], 'rw_g2': rw_g2[l],
              'rw_kk': rw_kk[l], 'rw_ka': rw_ka[l], 'rw_rk': rw_rk[l], 'rw_ln_w': rw_ln_w[l],
              'rw_ln_b': rw_ln_b[l], 'w_pa': w_pa[l], 'w_pb': w_pb[l], 'w_o': w_o[l],
              'norm2': norm2[l], 'w_router': w_router[l], 'b_router': b_router[l],
              'w_up': w_up[l], 'b_up': b_up[l], 'w_down': w_down[l], 'b_down': b_down[l]}
        hp, a1, a2, a3, a4 = hybrid_layer(
            hp, 0,
            jnp.zeros((B, 0, 4, NSA_KV_HEADS, HEAD_DIM), hp.dtype),
            jnp.zeros((B, 0, 2, NSA_KV_HEADS, HEAD_DIM), hp.dtype),
            jnp.zeros((B, RW_HEADS, RW_HEAD_DIM, RW_HEAD_DIM), jnp.float32),
            jnp.zeros((B, RW_COLS), jnp.float32),
            min(WINDOW, S), lp)
        kv_p.append(a1); win_p.append(a2); wkv_p.append(a3); sh_p.append(a4)
        past_rows = cache_kv[l][page_table].reshape((DB, past) + cache_kv.shape[3:])
        hs, b1, b2, b3, b4 = hybrid_layer(hs, past, past_rows, cache_win[l], state_wkv[l],
                                          state_shift[l], nbuf, lp)
        kv_s.append(b1); win_s.append(b2); wkv_s.append(b3); sh_s.append(b4)
    y_prompt = rmsnorm(hp, norm_f)
    y_sample = rmsnorm(hs, norm_f)
    return (y_prompt, y_sample, jnp.stack(kv_p), jnp.stack(kv_s), jnp.stack(win_p), jnp.stack(win_s),
            jnp.stack(wkv_p), jnp.stack(wkv_s), jnp.stack(sh_p), jnp.stack(sh_s))
```

```python
import functools
import math

import jax
import jax.numpy as jnp
import numpy as np
from jax import lax
from jax.experimental import pallas as pl
from jax.experimental.pallas import tpu as pltpu

F32 = jnp.float32
BF16 = jnp.bfloat16
HI = lax.Precision.HIGHEST

D_MODEL = 1024
NSA_HEADS = 8
NSA_KV_HEADS = 2
HEAD_DIM = 64
GROUP = NSA_HEADS // NSA_KV_HEADS
NSA_WIDTH = NSA_HEADS * HEAD_DIM
KV_LANES = NSA_KV_HEADS * HEAD_DIM
CMP_BLOCK = 32
CMP_STRIDE = 16
SEL_BLOCK = 64
SEL_TOP = 16
WINDOW = 512
SEL_FORCE = 1e9
NEG_INF = -1e30
PAGE = 128

RW_HEADS = 8
RW_HEAD_DIM = 64
RW_WIDTH = RW_HEADS * RW_HEAD_DIM
RW_DECAY_LORA = 64
RW_AAA_LORA = 64
RW_GATE_LORA = 128
RW_COLS = 3 * RW_WIDTH + RW_DECAY_LORA + RW_AAA_LORA + RW_GATE_LORA
GN_EPS = 64e-5

N_EXPERTS = 32
TOP_K = 4
D_FF = 1024
SWIGLU_LIMIT = 7.0
SWIGLU_ALPHA = 1.702
RMS_EPS = 1e-5

OFF_KV = NSA_WIDTH
OFF_GN = OFF_KV + 6 * KV_LANES
OFF_RW = OFF_GN + 3 * NSA_HEADS
OFF_MG = OFF_RW + RW_COLS
N_PROJ = OFF_MG + 2 * D_MODEL

LANES = 128
GN_PAD = LANES
P_Q = 0
P_KV = P_Q + NSA_WIDTH
P_GN = P_KV + 6 * KV_LANES
P_RW = P_GN + GN_PAD
P_MG = P_RW + RW_COLS
P_END = P_MG + 2 * D_MODEL

VMEM_LIMIT = 52 * 1024 * 1024

TOK_TILE = 256
TOK_ALIGN = 512
MOE_BM = 512
GATHER_CH = 2048
KEY_TILE = 128


def _cparams(sem):
    return pltpu.CompilerParams(dimension_semantics=sem, vmem_limit_bytes=VMEM_LIMIT)


def _sigmoid(x):
    return 1.0 / (1.0 + jnp.exp(-x))


def _round_up(a, b):
    return -(-a // b) * b


def _proj_kernel(x_ref, g_ref, w_ref, q_ref, kv_ref, kvb_ref, gn_ref, rw_ref, mg_ref):
    x = x_ref[...]
    ms = jnp.mean(x * x, axis=-1, keepdims=True)
    xn = (x * lax.rsqrt(ms + RMS_EPS) * g_ref[...]).astype(BF16)

    def seg(a, b):
        return jnp.dot(xn, w_ref[:, a:b], preferred_element_type=F32)

    q_ref[...] = (seg(P_Q, P_KV) * (HEAD_DIM ** -0.5)).astype(BF16)
    kv = seg(P_KV, P_GN)
    kv_ref[...] = kv
    kvb_ref[...] = kv.astype(BF16)
    gn_ref[...] = _sigmoid(seg(P_GN, P_RW))
    rw_ref[...] = seg(P_RW, P_MG)
    mg_ref[...] = _sigmoid(seg(P_MG, P_END))


def _proj(x, norm1, w_pad):
    T = x.shape[0]
    tm = TOK_TILE
    row = lambda i: (i, 0)
    const = lambda i: (0, 0)
    outs = (
        jax.ShapeDtypeStruct((T, NSA_WIDTH), BF16),
        jax.ShapeDtypeStruct((T, 6 * KV_LANES), F32),
        jax.ShapeDtypeStruct((T, 6 * KV_LANES), BF16),
        jax.ShapeDtypeStruct((T, GN_PAD), F32),
        jax.ShapeDtypeStruct((T, RW_COLS), F32),
        jax.ShapeDtypeStruct((T, 2 * D_MODEL), F32),
    )
    return pl.pallas_call(
        _proj_kernel,
        grid=(T // tm,),
        in_specs=[pl.BlockSpec((tm, D_MODEL), row), pl.BlockSpec((1, D_MODEL), const),
                  pl.BlockSpec((D_MODEL, P_END), const)],
        out_specs=[pl.BlockSpec((tm, o.shape[1]), row) for o in outs],
        out_shape=outs,
        compiler_params=_cparams(("parallel",)),
        name="proj",
    )(x, norm1.reshape(1, D_MODEL), w_pad)


def _pagecopy_kernel(pt_ref, cache_ref, tail_ref, out_ref, sem, *, n_pages):
    b = pl.program_id(0)

    def page_copy(p):
        return pltpu.make_async_copy(cache_ref.at[pt_ref[b, p]], out_ref.at[b, pl.ds(p * PAGE, PAGE)], sem)

    def tail_copy():
        return pltpu.make_async_copy(tail_ref.at[b], out_ref.at[b, pl.ds(n_pages * PAGE, PAGE)], sem)

    def start(p, c):
        page_copy(p).start()
        return c

    def wait(p, c):
        page_copy(p).wait()
        return c

    lax.fori_loop(0, n_pages, start, 0)
    tail_copy().start()
    lax.fori_loop(0, n_pages, wait, 0)
    tail_copy().wait()


def _pagecopy(page_table, cache2d, tail):
    db, n_pages = page_table.shape
    w = cache2d.shape[-1]
    any_spec = pl.BlockSpec(memory_space=pl.ANY)
    return pl.pallas_call(
        functools.partial(_pagecopy_kernel, n_pages=n_pages),
        grid_spec=pltpu.PrefetchScalarGridSpec(
            num_scalar_prefetch=1, grid=(db,), in_specs=[any_spec, any_spec], out_specs=any_spec,
            scratch_shapes=[pltpu.SemaphoreType.DMA(())]),
        out_shape=jax.ShapeDtypeStruct((db, (n_pages + 1) * PAGE, w), cache2d.dtype),
        compiler_params=_cparams(("arbitrary",)),
        name="pagecopy",
    )(page_table, cache2d, tail)


def _gelu_tanh(x):
    return 0.5 * x * (1.0 + jnp.tanh(math.sqrt(2.0 / math.pi) * (x + 0.044715 * (x * x * x))))


def _compress_kernel(rk_ref, rv_ref, mix_ref, pe_ref, w1_ref, w2_ref, kc_ref, vc_ref):
    nch = kc_ref.shape[1]
    for idx, (r_ref, o_ref) in enumerate(((rk_ref, kc_ref), (rv_ref, vc_ref))):
        mix = mix_ref[idx]
        cs0 = jnp.zeros((nch, LANES), F32)
        cs1 = jnp.zeros((nch, LANES), F32)
        for p in range(CMP_STRIDE):
            xp = r_ref[0, pl.ds(p, nch, stride=CMP_STRIDE), :].astype(F32)
            cs0 = cs0 + xp * mix[p:p + 1]
            cs1 = cs1 + xp * mix[CMP_STRIDE + p:CMP_STRIDE + p + 1]
        c0 = jnp.sum(mix * pe_ref[idx], axis=0, keepdims=True)
        pre = c0 + cs0 + pltpu.roll(cs1, nch - 1, 0)
        h = _gelu_tanh(jnp.dot(pre, w1_ref[idx], precision=HI, preferred_element_type=F32))
        o_ref[0] = jnp.dot(h, w2_ref[idx], precision=HI, preferred_element_type=F32)


def _compress(rows3d, mix2, pe2, w1bd, w2bd):
    b, l, _ = rows3d.shape
    nch = l // CMP_STRIDE
    const3 = lambda i: (0, 0, 0)
    out = jax.ShapeDtypeStruct((b, nch, LANES), F32)
    return pl.pallas_call(
        _compress_kernel,
        grid=(b,),
        in_specs=[pl.BlockSpec((1, l, LANES), lambda i: (i, 0, 0)),
                  pl.BlockSpec((1, l, LANES), lambda i: (i, 0, 1)),
                  pl.BlockSpec((2, CMP_BLOCK, LANES), const3), pl.BlockSpec((2, CMP_BLOCK, LANES), const3),
                  pl.BlockSpec((2, LANES, LANES), const3), pl.BlockSpec((2, LANES, LANES), const3)],
        out_specs=[pl.BlockSpec((1, nch, LANES), lambda i: (i, 0, 0))] * 2,
        out_shape=(out, out),
        compiler_params=_cparams(("parallel",)),
        name="compress",
    )(rows3d, rows3d, mix2, pe2, w1bd, w2bd)


def _masked_softmax(s, mask):
    s = jnp.where(mask, s, NEG_INF)
    m = jnp.max(s, axis=-1, keepdims=True)
    e = jnp.where(mask, jnp.exp(s - m), 0.0)
    return e / jnp.maximum(jnp.sum(e, axis=-1, keepdims=True), 1e-30)


_NT = (((1,), (1,)), ((), ()))


def _flash(qb, k_ref, v_ref, lo, hi, mask_fn, kt_rows):
    r = qb.shape[0]

    def body(kt, carry):
        m, l, acc = carry
        off = pl.multiple_of(kt * kt_rows, kt_rows)
        k = k_ref[0, pl.ds(off, kt_rows), :].astype(BF16)
        v = v_ref[0, pl.ds(off, kt_rows), :].astype(BF16)
        s = lax.dot_general(qb, k, _NT, preferred_element_type=F32)
        mask = mask_fn(kt)
        s = jnp.where(mask, s, NEG_INF)
        m_new = jnp.maximum(m, jnp.max(s, axis=-1, keepdims=True))
        p = jnp.where(mask, jnp.exp(s - m_new), 0.0)
        alpha = jnp.exp(m - m_new)
        l = alpha * l + jnp.sum(p, axis=-1, keepdims=True)
        acc = alpha * acc + jnp.dot(p.astype(BF16), v, preferred_element_type=F32)
        return m_new, l, acc

    init = (jnp.full((r, 1), NEG_INF, F32), jnp.zeros((r, 1), F32), jnp.zeros((r, LANES), F32))
    _, l, acc = lax.fori_loop(lo, hi, body, init)
    return acc / jnp.maximum(l, 1e-30)


def _nsa_kernel(qbd_ref, gn_ref, kc_ref, vc_ref, cover_ref, ks_ref, vs_ref, kw_ref, vw_ref, e_ref, o_ref,
                *, tq, q_start, kw_start, n_top):
    kt_rows = KEY_TILE
    qi = pl.program_id(1)
    r = NSA_HEADS * tq
    t0 = q_start + qi * tq
    qb = qbd_ref[0, 0]
    t_row = t0 + (lax.broadcasted_iota(jnp.int32, (r, 1), 0) & (tq - 1))

    kc = kc_ref[0].astype(BF16)
    vc = vc_ref[0].astype(BF16)
    nch = kc.shape[0]
    s_c = lax.dot_general(qb, kc, _NT, preferred_element_type=F32)
    cmp_end = lax.broadcasted_iota(jnp.int32, (1, nch), 1) * CMP_STRIDE + (CMP_BLOCK - 1)
    p_c = _masked_softmax(s_c, cmp_end <= t_row)
    o_c = jnp.dot(p_c.astype(BF16), vc, preferred_element_type=F32)

    ns_pad = cover_ref.shape[1]
    tq_pos = t0 + lax.broadcasted_iota(jnp.int32, (tq, 1), 0)
    cur = tq_pos >> 6
    jb = lax.broadcasted_iota(jnp.int32, (1, ns_pad), 1)
    jbf = jb.astype(F32)
    elig = jb <= cur
    forced = (jb == 0) | (jb == cur) | (jb == cur - 1)
    sels = []
    for g in range(NSA_KV_HEADS):
        base = g * GROUP * tq
        pg = p_c[base:base + tq]
        for n in range(1, GROUP):
            pg = pg + p_c[base + n * tq:base + (n + 1) * tq]
        imp = jnp.dot(pg, cover_ref[...], precision=HI, preferred_element_type=F32)
        score = jnp.where(elig, jnp.where(forced, SEL_FORCE, imp), -SEL_FORCE)
        sel = jnp.zeros((tq, ns_pad), F32)
        for _ in range(n_top):
            m = jnp.max(score, axis=-1, keepdims=True)
            first = jnp.min(jnp.where(score == m, jbf, 1e9), axis=-1, keepdims=True)
            hit = jbf == first
            sel = jnp.where(hit, 1.0, sel)
            score = jnp.where(hit, -3e38, score)
        sels.append(jnp.where(elig, sel, 0.0))
    selrows = jnp.concatenate([sels[0]] * GROUP + [sels[1]] * GROUP, axis=0).astype(BF16)

    def sel_mask(kt):
        blk = kt * (kt_rows // SEL_BLOCK) + (lax.broadcasted_iota(jnp.int32, (ns_pad, kt_rows), 1) >> 6)
        expand = jnp.where(lax.broadcasted_iota(jnp.int32, (ns_pad, kt_rows), 0) == blk, 1.0, 0.0).astype(BF16)
        selx = jnp.dot(selrows, expand, preferred_element_type=F32)
        kpos = kt * kt_rows + lax.broadcasted_iota(jnp.int32, (1, kt_rows), 1)
        return (selx > 0.5) & (kpos <= t_row)

    hi_s = (t0 + tq - 1) // kt_rows + 1
    o_s = _flash(qb, ks_ref, vs_ref, 0, hi_s, sel_mask, kt_rows)

    def win_mask(kt):
        kpos = kw_start + kt * kt_rows + lax.broadcasted_iota(jnp.int32, (1, kt_rows), 1)
        dist = t_row - kpos
        return (dist >= 0) & (dist <= WINDOW)

    n_wt = kw_ref.shape[1] // kt_rows
    lo_w = jnp.maximum(t0 - WINDOW - kw_start, 0) // kt_rows
    hi_w = jnp.minimum((t0 + tq - 1 - kw_start) // kt_rows + 1, n_wt)
    o_w = _flash(qb, kw_ref, vw_ref, lo_w, hi_w, win_mask, kt_rows)

    lane = lax.broadcasted_iota(jnp.int32, (tq, LANES), 1)

    def assemble(o):
        pieces = []
        for j in range(NSA_HEADS // 2):
            a = o[(2 * j) * tq:(2 * j + 1) * tq]
            b = o[(2 * j + 1) * tq:(2 * j + 2) * tq]
            if j < GROUP // 2:
                x, y = a, pltpu.roll(b, HEAD_DIM, 1)
            else:
                x, y = pltpu.roll(a, HEAD_DIM, 1), b
            pieces.append(jnp.where(lane < HEAD_DIM, x, y))
        return jnp.concatenate(pieces, axis=1)

    gn = gn_ref[0]
    out = None
    for br, o in enumerate((o_c, o_s, o_w)):
        gate = jnp.dot(gn, e_ref[br], precision=HI, preferred_element_type=F32)
        term = gate * assemble(o)
        out = term if out is None else out + term
    o_ref[0] = out.astype(o_ref.dtype)


def _cover_matrix(nch, ns_pad):
    ci = np.arange(nch)[:, None] * CMP_STRIDE
    sj = np.arange(ns_pad)[None, :] * SEL_BLOCK
    return jnp.asarray(((ci < sj + SEL_BLOCK) & (ci + CMP_BLOCK > sj)).astype(np.float32))


def _gate_expand():
    e = np.zeros((3, GN_PAD, NSA_WIDTH), np.float32)
    for br in range(3):
        for h in range(NSA_HEADS):
            e[br, br * NSA_HEADS + h, h * HEAD_DIM:(h + 1) * HEAD_DIM] = 1.0
    return jnp.asarray(e)


def _block_diag_q(q, b, sq, tq):
    nqt = sq // tq
    qr = q.reshape(b, nqt, tq, NSA_KV_HEADS, GROUP, HEAD_DIM).transpose(0, 1, 3, 4, 2, 5)
    z = jnp.zeros_like(qr[:, :, 0])
    g0 = jnp.concatenate([qr[:, :, 0], z], axis=-1)
    g1 = jnp.concatenate([z, qr[:, :, 1]], axis=-1)
    return jnp.stack([g0, g1], axis=2).reshape(b, nqt, NSA_HEADS * tq, LANES)


def _nsa(qbd, gn3, kc, vc, ksv, ks_col, kwv, kw_col, *, tq, q_start, kw_start, ns_true):
    b, nqt = qbd.shape[:2]
    sq = nqt * tq
    nch = kc.shape[1]
    lk = ksv.shape[1]
    lw = kwv.shape[1]
    ns_pad = _round_up(max(lk // SEL_BLOCK, LANES), LANES)
    cover = _cover_matrix(nch, ns_pad)
    n_top = min(SEL_TOP, ns_true)
    kern = functools.partial(_nsa_kernel, tq=tq, q_start=q_start, kw_start=kw_start, n_top=n_top)
    per_b = lambda i, j: (i, 0, 0)
    return pl.pallas_call(
        kern,
        grid=(b, nqt),
        in_specs=[pl.BlockSpec((1, 1, NSA_HEADS * tq, LANES), lambda i, j: (i, j, 0, 0)),
                  pl.BlockSpec((1, tq, GN_PAD), lambda i, j: (i, j, 0)),
                  pl.BlockSpec((1, nch, LANES), per_b), pl.BlockSpec((1, nch, LANES), per_b),
                  pl.BlockSpec((nch, ns_pad), lambda i, j: (0, 0)),
                  pl.BlockSpec((1, lk, LANES), lambda i, j: (i, 0, ks_col)),
                  pl.BlockSpec((1, lk, LANES), lambda i, j: (i, 0, ks_col + 1)),
                  pl.BlockSpec((1, lw, LANES), lambda i, j: (i, 0, kw_col)),
                  pl.BlockSpec((1, lw, LANES), lambda i, j: (i, 0, kw_col + 1)),
                  pl.BlockSpec((3, GN_PAD, NSA_WIDTH), lambda i, j: (0, 0, 0))],
        out_specs=pl.BlockSpec((1, tq, NSA_WIDTH), lambda i, j: (i, j, 0)),
        out_shape=jax.ShapeDtypeStruct((b, sq, NSA_WIDTH), BF16),
        compiler_params=_cparams(("parallel", "arbitrary")),
        name="nsa",
    )(qbd, gn3, kc, vc, cover, ksv, ksv, kwv, kwv, _gate_expand())


def _rwkv_pre_kernel(c_ref, p_ref, mu_ref, w0_ref, a0_ref, kkw_ref, ka_ref, wl_ref, g2_ref,
                     r_ref, k_ref, v_ref, kk_ref, a_ref, d_ref, g_ref):
    cols = c_ref[...]
    xs = cols + mu_ref[...] * (p_ref[...] - cols)
    o1, o2, o3 = RW_WIDTH, 2 * RW_WIDTH, 3 * RW_WIDTH
    o5 = o3 + RW_DECAY_LORA + RW_AAA_LORA
    k = xs[:, o1:o2]
    wa = xs[:, o3:o5]
    lane = lax.broadcasted_iota(jnp.int32, wa.shape, 1)
    wa = jnp.where(lane < RW_DECAY_LORA, jnp.tanh(wa), wa)
    lo = jnp.dot(wa.astype(BF16), wl_ref[...], preferred_element_type=F32)
    z = -(w0_ref[...] + lo[:, :RW_WIDTH])
    softplus = jnp.maximum(z, 0.0) + jnp.log(1.0 + jnp.exp(-jnp.abs(z)))
    w = -softplus - 0.5
    a = _sigmoid(a0_ref[...] + lo[:, RW_WIDTH:])
    r_ref[...] = xs[:, :o1]
    k_ref[...] = k * (1.0 + (a - 1.0) * ka_ref[...])
    v_ref[...] = xs[:, o2:o3]
    kk_ref[...] = k * kkw_ref[...]
    a_ref[...] = a
    d_ref[...] = jnp.exp(-jnp.exp(w))
    g_ref[...] = jnp.dot(_sigmoid(xs[:, o5:]).astype(BF16), g2_ref[...], preferred_element_type=F32)


def _rwkv_pre(cols, prev, mu, w0, a0, kkw, ka, wlora, g2):
    t = cols.shape[0]
    tm = TOK_TILE
    row = lambda i: (i, 0)
    const = lambda i: (0, 0)
    vec = pl.BlockSpec((1, RW_WIDTH), const)
    out = jax.ShapeDtypeStruct((t, RW_WIDTH), F32)
    return pl.pallas_call(
        _rwkv_pre_kernel,
        grid=(t // tm,),
        in_specs=[pl.BlockSpec((tm, RW_COLS), row), pl.BlockSpec((tm, RW_COLS), row),
                  pl.BlockSpec((1, RW_COLS), const), vec, vec, vec, vec,
                  pl.BlockSpec((RW_DECAY_LORA + RW_AAA_LORA, 2 * RW_WIDTH), const),
                  pl.BlockSpec((RW_GATE_LORA, RW_WIDTH), const)],
        out_specs=[pl.BlockSpec((tm, RW_WIDTH), row)] * 7,
        out_shape=(out,) * 7,
        compiler_params=_cparams(("parallel",)),
        name="rwkv_pre",
    )(cols, prev, mu, w0, a0, kkw, ka, wlora, g2)


def _rwkv_scan_kernel(r_ref, k_ref, v_ref, kk_ref, a_ref, d_ref, g_ref, s0_ref, lnw_ref, lnb_ref, rk_ref,
                      o_ref, sf_ref, st_ref, kkn_ref, b_ref):
    n = RW_HEAD_DIM
    tb = r_ref.shape[0]
    ti = pl.program_id(1)

    @pl.when(ti == 0)
    def _():
        st_ref[...] = s0_ref[...]

    def step(t, c):
        kk = kk_ref[t]
        nrm = jnp.sqrt(jnp.sum(kk * kk, axis=0, keepdims=True))
        kkn = kk / jnp.maximum(nrm, 1e-12)
        kkn_ref[...] = kkn
        b_ref[...] = kkn * a_ref[t]
        vv = v_ref[t]
        sa = jnp.zeros((n, LANES), F32)
        for i in range(n):
            sa = sa - st_ref[i] * kkn_ref[pl.ds(i, 1), :]
        y = jnp.zeros((n, LANES), F32)
        for i in range(n):
            s_new = (st_ref[i] * d_ref[t, pl.ds(i, 1), :] + sa * b_ref[pl.ds(i, 1), :]
                     + vv * k_ref[t, pl.ds(i, 1), :])
            st_ref[i] = s_new
            y = y + s_new * r_ref[t, pl.ds(i, 1), :]
        mu = jnp.mean(y, axis=0, keepdims=True)
        yc = y - mu
        var = jnp.mean(yc * yc, axis=0, keepdims=True)
        yn = yc * lax.rsqrt(var + GN_EPS) * lnw_ref[...] + lnb_ref[...]
        bonus = jnp.sum(r_ref[t] * k_ref[t] * rk_ref[...], axis=0, keepdims=True) * vv
        o_ref[t] = (yn + bonus) * g_ref[t]
        return c

    lax.fori_loop(0, tb, step, 0)

    @pl.when(ti == pl.num_programs(1) - 1)
    def _():
        sf_ref[...] = st_ref[...]


def _rwkv_scan(seqs, s0, lnw, lnb, rk, tb):
    s, n, nl = seqs[0].shape
    blk = pl.BlockSpec((tb, n, LANES), lambda j, i: (i, 0, j))
    st = pl.BlockSpec((n, n, LANES), lambda j, i: (0, 0, j))
    vec = pl.BlockSpec((n, LANES), lambda j, i: (0, j))
    return pl.pallas_call(
        _rwkv_scan_kernel,
        grid=(nl // LANES, s // tb),
        in_specs=[blk] * 7 + [st, vec, vec, vec],
        out_specs=[blk, st],
        out_shape=(jax.ShapeDtypeStruct((s, n, nl), F32), jax.ShapeDtypeStruct((n, n, nl), F32)),
        scratch_shapes=[pltpu.VMEM((n, n, LANES), F32), pltpu.VMEM((n, LANES), F32), pltpu.VMEM((n, LANES), F32)],
        compiler_params=_cparams(("parallel", "arbitrary")),
        name="rwkv_scan",
    )(*seqs, s0, lnw, lnb, rk)


def _to_scan_layout(x, b, s):
    return x.reshape(b, s, RW_HEADS, RW_HEAD_DIM).transpose(1, 3, 0, 2).reshape(s, RW_HEAD_DIM, b * RW_HEADS)


def _from_scan_layout(y, b, s):
    return y.reshape(s, RW_HEAD_DIM, b, RW_HEADS).transpose(2, 0, 3, 1).reshape(b * s, RW_WIDTH)


def _rwkv_group(pre, b, s, wkv0, lnw, lnb, rk):
    nl = b * RW_HEADS
    nlp = _round_up(nl, LANES)

    def padl(x):
        return x if nlp == nl else jnp.pad(x, [(0, 0)] * (x.ndim - 1) + [(0, nlp - nl)])

    seqs = [padl(_to_scan_layout(x, b, s)) for x in pre]
    s0 = padl(wkv0.astype(F32).transpose(3, 2, 0, 1).reshape(RW_HEAD_DIM, RW_HEAD_DIM, nl))
    tile = lambda p: padl(jnp.tile(p.reshape(RW_HEADS, RW_HEAD_DIM).T, (1, b)))
    tb = math.gcd(s, 32)
    o, sf = _rwkv_scan(seqs, s0, tile(lnw), tile(lnb), tile(rk), tb)
    o = _from_scan_layout(o[:, :, :nl], b, s)
    sf = sf[:, :, :nl].reshape(RW_HEAD_DIM, RW_HEAD_DIM, b, RW_HEADS).transpose(2, 3, 1, 0)
    return o, sf


def _merge_kernel(x_ref, on_ref, or_ref, mg_ref, wpa_ref, wpb_ref, wo_ref, n2_ref, wr_ref, br_ref,
                  h_ref, xn_ref, te_ref, tg_ref):
    pa = jnp.dot(on_ref[...], wpa_ref[...], preferred_element_type=F32)
    pb = jnp.dot(or_ref[...].astype(BF16), wpb_ref[...], preferred_element_type=F32)
    merged = mg_ref[:, :D_MODEL] * pa + mg_ref[:, D_MODEL:] * pb
    h = x_ref[...] + jnp.dot(merged.astype(BF16), wo_ref[...], preferred_element_type=F32)
    h_ref[...] = h
    ms = jnp.mean(h * h, axis=-1, keepdims=True)
    xn = h * lax.rsqrt(ms + RMS_EPS) * n2_ref[...]
    xn_ref[...] = xn
    logits = jnp.dot(xn, wr_ref[...], precision=HI, preferred_element_type=F32) + br_ref[...]
    lane = lax.broadcasted_iota(jnp.int32, logits.shape, 1)
    lanef = lane.astype(F32)
    te = jnp.zeros(logits.shape, F32)
    tv = jnp.full(logits.shape, NEG_INF, F32)
    for kx in range(TOP_K):
        m = jnp.max(logits, axis=-1, keepdims=True)
        first = jnp.min(jnp.where(logits == m, lanef, 1e9), axis=-1, keepdims=True)
        te = jnp.where(lane == kx, first, te)
        tv = jnp.where(lane == kx, m, tv)
        logits = jnp.where(lanef == first, -3e38, logits)
    e = jnp.where(lane < TOP_K, jnp.exp(tv - jnp.max(tv, axis=-1, keepdims=True)), 0.0)
    te_ref[...] = te.astype(jnp.int32)
    tg_ref[...] = e / jnp.sum(e, axis=-1, keepdims=True)


def _merge(x, o_nsa, o_rw, mg, wpa, wpb, wo, norm2, wr_pad, br_pad):
    t = x.shape[0]
    tm = TOK_TILE
    row = lambda i: (i, 0)
    const = lambda i: (0, 0)
    outs = (jax.ShapeDtypeStruct((t, D_MODEL), F32), jax.ShapeDtypeStruct((t, D_MODEL), F32),
            jax.ShapeDtypeStruct((t, LANES), jnp.int32), jax.ShapeDtypeStruct((t, LANES), F32))
    return pl.pallas_call(
        _merge_kernel,
        grid=(t // tm,),
        in_specs=[pl.BlockSpec((tm, D_MODEL), row), pl.BlockSpec((tm, NSA_WIDTH), row),
                  pl.BlockSpec((tm, RW_WIDTH), row), pl.BlockSpec((tm, 2 * D_MODEL), row),
                  pl.BlockSpec((NSA_WIDTH, D_MODEL), const), pl.BlockSpec((RW_WIDTH, D_MODEL), const),
                  pl.BlockSpec((D_MODEL, D_MODEL), const), pl.BlockSpec((1, D_MODEL), const),
                  pl.BlockSpec((D_MODEL, LANES), const), pl.BlockSpec((1, LANES), const)],
        out_specs=[pl.BlockSpec((tm, o.shape[1]), row) for o in outs],
        out_shape=outs,
        compiler_params=_cparams(("parallel",)),
        name="merge",
    )(x, o_nsa, o_rw, mg, wpa, wpb, wo, norm2.reshape(1, D_MODEL), wr_pad, br_pad)


def _rowgather_kernel(idx_ref, src_ref, out_ref, sem, *, ch):
    base = pl.program_id(0) * ch

    def row_copy(j):
        return pltpu.make_async_copy(src_ref.at[pl.ds(idx_ref[j], 1)], out_ref.at[pl.ds(base + j, 1)], sem)

    def start(j, c):
        row_copy(j).start()
        return c

    def wait(j, c):
        row_copy(j).wait()
        return c

    lax.fori_loop(0, ch, start, 0)
    lax.fori_loop(0, ch, wait, 0)


def _rowgather(idx, src):
    n = idx.shape[0]
    ch = GATHER_CH
    any_spec = pl.BlockSpec(memory_space=pl.ANY)
    return pl.pallas_call(
        functools.partial(_rowgather_kernel, ch=ch),
        grid=(n // ch,),
        in_specs=[pl.BlockSpec((ch,), lambda i: (i,), memory_space=pltpu.SMEM), any_spec],
        out_specs=any_spec,
        out_shape=jax.ShapeDtypeStruct((n, src.shape[1]), src.dtype),
        scratch_shapes=[pltpu.SemaphoreType.DMA(())],
        compiler_params=_cparams(("arbitrary",)),
        name="rowgather",
    )(idx, src)


def _expert_kernel(be_ref, nu_ref, x_ref, wu_ref, bu_ref, wd_ref, bd_ref, o_ref):
    i = pl.program_id(0)

    @pl.when(i < nu_ref[0])
    def _():
        x = x_ref[...].astype(BF16)
        hu = jnp.dot(x, wu_ref[0], preferred_element_type=F32) + bu_ref[0]
        gt = jnp.minimum(hu[:, :D_FF], SWIGLU_LIMIT)
        up = jnp.clip(hu[:, D_FF:], -SWIGLU_LIMIT, SWIGLU_LIMIT)
        hh = (up + 1.0) * gt * _sigmoid(SWIGLU_ALPHA * gt)
        o_ref[...] = jnp.dot(hh.astype(BF16), wd_ref[0], preferred_element_type=F32) + bd_ref[0]

    @pl.when(i >= nu_ref[0])
    def _():
        o_ref[...] = jnp.zeros(o_ref.shape, o_ref.dtype)


def _experts(blk_exp, n_used, xs, w_up, b_up, w_down, b_down):
    p = xs.shape[0]
    bm = MOE_BM
    wmap = lambda i, be, nu: (be[i], 0, 0)
    return pl.pallas_call(
        _expert_kernel,
        grid_spec=pltpu.PrefetchScalarGridSpec(
            num_scalar_prefetch=2, grid=(p // bm,),
            in_specs=[pl.BlockSpec((bm, D_MODEL), lambda i, be, nu: (i, 0)),
                      pl.BlockSpec((1, D_MODEL, 2 * D_FF), wmap), pl.BlockSpec((1, 1, 2 * D_FF), wmap),
                      pl.BlockSpec((1, D_FF, D_MODEL), wmap), pl.BlockSpec((1, 1, D_MODEL), wmap)],
            out_specs=pl.BlockSpec((bm, D_MODEL), lambda i, be, nu: (i, 0))),
        out_shape=jax.ShapeDtypeStruct((p, D_MODEL), F32),
        compiler_params=_cparams(("arbitrary",)),
        name="experts",
    )(blk_exp, n_used, xs, w_up, b_up.reshape(N_EXPERTS, 1, 2 * D_FF), w_down, b_down.reshape(N_EXPERTS, 1, D_MODEL))


def _final_kernel(h_ref, y_ref, tg_ref, nf_ref, o_ref):
    tg = tg_ref[...]
    f = tg[:, 0:1] * y_ref[0]
    for kx in range(1, TOP_K):
        f = f + tg[:, kx:kx + 1] * y_ref[kx]
    y = h_ref[...] + f
    ms = jnp.mean(y * y, axis=-1, keepdims=True)
    o_ref[...] = y * lax.rsqrt(ms + RMS_EPS) * nf_ref[...]


def _final(h, yk, tg, norm_f):
    t = h.shape[0]
    tm = TOK_TILE
    row = lambda i: (i, 0)
    return pl.pallas_call(
        _final_kernel,
        grid=(t // tm,),
        in_specs=[pl.BlockSpec((tm, D_MODEL), row), pl.BlockSpec((TOP_K, tm, D_MODEL), lambda i: (0, i, 0)),
                  pl.BlockSpec((tm, LANES), row), pl.BlockSpec((1, D_MODEL), lambda i: (0, 0))],
        out_specs=pl.BlockSpec((tm, D_MODEL), row),
        out_shape=jax.ShapeDtypeStruct((t, D_MODEL), F32),
        compiler_params=_cparams(("parallel",)),
        name="final",
    )(h, yk, tg, norm_f.reshape(1, D_MODEL))


def _routing(top_e, bm):
    t = top_e.shape[0]
    a = t * TOP_K
    e_flat = top_e.reshape(a)
    order = jnp.argsort(e_flat).astype(jnp.int32)
    e_sorted = e_flat[order]
    counts = jnp.bincount(e_flat, length=N_EXPERTS).astype(jnp.int32)
    starts = jnp.cumsum(counts) - counts
    padded = (counts + bm - 1) // bm * bm
    pends = jnp.cumsum(padded)
    pstarts = pends - padded
    slot_sorted = (pstarts[e_sorted] + jnp.arange(a, dtype=jnp.int32) - starts[e_sorted]).astype(jnp.int32)
    n_blocks = _round_up(-(-a // bm) + N_EXPERTS, GATHER_CH // bm)
    p = n_blocks * bm
    slot_tok = jnp.zeros((p,), jnp.int32).at[slot_sorted].set(order // TOP_K)
    slot_of = jnp.zeros((a,), jnp.int32).at[order].set(slot_sorted).reshape(t, TOP_K)
    blk_exp = jnp.minimum(jnp.sum(jnp.arange(n_blocks)[:, None] * bm >= pends[None, :], axis=1), N_EXPERTS - 1)
    n_used = (pends[-1:] // bm).astype(jnp.int32)
    return slot_tok, slot_of, blk_exp.astype(jnp.int32), n_used


def kernel(x_prompt, x_sample, cache_kv, cache_win, state_wkv, state_shift, page_table, norm1, w_in, cmp_pe, cmp_mix, cmp_w1, cmp_w2, rw_mu, rw_w0, rw_w2, rw_a0, rw_a2, rw_g2, rw_kk, rw_ka, rw_rk, rw_ln_w, rw_ln_b, w_pa, w_pb, w_o, norm2, w_router, b_router, w_up, b_up, w_down, b_down, norm_f):
    assert w_in.shape[0] == 1, "single-layer trunk"
    b, s, _ = x_prompt.shape
    db, ds, _ = x_sample.shape
    n_pages = page_table.shape[1]
    assert cache_kv.shape[2] == PAGE
    past = n_pages * PAGE
    nbuf = cache_win.shape[2]
    tp, ts = b * s, db * ds
    t_real = tp + ts
    t_pad = _round_up(t_real, TOK_ALIGN)

    wi = w_in[0]
    w_pad = jnp.concatenate(
        [wi[:, :OFF_GN], jnp.pad(wi[:, OFF_GN:OFF_RW], ((0, 0), (0, GN_PAD - 3 * NSA_HEADS))), wi[:, OFF_RW:]],
        axis=1).astype(BF16)
    tile_g = lambda p: jnp.tile(p, (1, 1, NSA_KV_HEADS))
    mix2, pe2 = tile_g(cmp_mix[0].astype(F32)), tile_g(cmp_pe[0].astype(F32))
    bd = lambda w: jnp.stack([jnp.kron(jnp.eye(NSA_KV_HEADS, dtype=F32), w[i].astype(F32)) for i in range(2)])
    w1bd, w2bd = bd(cmp_w1[0]), bd(cmp_w2[0])
    zl = jnp.zeros((RW_DECAY_LORA, RW_WIDTH), F32)
    wlora = jnp.concatenate([jnp.concatenate([rw_w2[0], zl], axis=1),
                             jnp.concatenate([zl, rw_a2[0]], axis=1)], axis=0).astype(BF16)
    vec = lambda p: p[0].reshape(1, -1).astype(F32)
    wr_pad = jnp.pad(w_router[0].astype(F32), ((0, 0), (0, LANES - N_EXPERTS)))
    br_pad = jnp.concatenate([b_router[0].astype(F32), jnp.full((LANES - N_EXPERTS,), NEG_INF, F32)]).reshape(1, LANES)

    x_all = jnp.concatenate([x_prompt.reshape(tp, D_MODEL), x_sample.reshape(ts, D_MODEL)], axis=0)
    x_all = jnp.pad(x_all, ((0, t_pad - t_real), (0, 0)))
    q_all, kv_all, kvb_all, gn_all, rw_all, mg_all = _proj(x_all, norm1[0], w_pad)

    kv_p = kv_all[:tp].reshape(b, s, 6 * KV_LANES)
    kv_s = kv_all[tp:t_real].reshape(db, ds, 6 * KV_LANES)
    kvb_p = kvb_all[:tp].reshape(b, s, 6 * KV_LANES)

    assert s % KEY_TILE == 0 and ds <= PAGE
    tq_p = math.gcd(s, 128)
    kc_p, vc_p = _compress(kv_p, mix2, pe2, w1bd, w2bd)
    o_nsa_p = _nsa(_block_diag_q(q_all[:tp], b, s, tq_p), gn_all[:tp].reshape(b, s, GN_PAD), kc_p, vc_p,
                   kvb_p, 2, kvb_p, 4, tq=tq_p, q_start=0, kw_start=0, ns_true=s // SEL_BLOCK)

    row_w = 4 * KV_LANES
    tail = jnp.pad(kv_s[:, :, :row_w], ((0, 0), (0, PAGE - ds), (0, 0)))
    kvfull_s = _pagecopy(page_table, cache_kv[0].reshape(-1, PAGE, row_w), tail)
    kc_s, vc_s = _compress(kvfull_s, mix2, pe2, w1bd, w2bd)
    kw_full = jnp.concatenate([cache_win[0].reshape(db, nbuf, 2 * KV_LANES), kv_s[:, :, row_w:]], axis=1)
    lw = _round_up(nbuf + ds, KEY_TILE)
    kw_pad = jnp.pad(kw_full, ((0, 0), (0, lw - (nbuf + ds)), (0, 0)))
    lp = _round_up(past + ds, SEL_BLOCK)
    o_nsa_s = _nsa(_block_diag_q(q_all[tp:t_real], db, ds, ds), gn_all[tp:t_real].reshape(db, ds, GN_PAD), kc_s, vc_s,
                   kvfull_s, 2, kw_pad, 0, tq=ds, q_start=past, kw_start=past - nbuf, ns_true=lp // SEL_BLOCK)
    o_nsa = jnp.concatenate([o_nsa_p.reshape(tp, NSA_WIDTH), o_nsa_s.reshape(ts, NSA_WIDTH),
                             jnp.zeros((t_pad - t_real, NSA_WIDTH), BF16)], axis=0)

    rw_p = rw_all[:tp].reshape(b, s, RW_COLS)
    rw_s = rw_all[tp:t_real].reshape(db, ds, RW_COLS)
    prev_p = jnp.concatenate([jnp.zeros((b, 1, RW_COLS), F32), rw_p[:, :-1]], axis=1)
    prev_s = jnp.concatenate([state_shift[0][:, None].astype(F32), rw_s[:, :-1]], axis=1)
    prev_all = jnp.concatenate([prev_p.reshape(tp, RW_COLS), prev_s.reshape(ts, RW_COLS),
                                jnp.zeros((t_pad - t_real, RW_COLS), F32)], axis=0)
    pre = _rwkv_pre(rw_all, prev_all, vec(rw_mu), vec(rw_w0), vec(rw_a0), vec(rw_kk), vec(rw_ka), wlora,
                    rw_g2[0].astype(BF16))
    o_rw_p, wkv_p = _rwkv_group([x[:tp] for x in pre], b, s,
                                jnp.zeros((b, RW_HEADS, RW_HEAD_DIM, RW_HEAD_DIM), F32),
                                rw_ln_w[0], rw_ln_b[0], rw_rk[0])
    o_rw_s, wkv_s = _rwkv_group([x[tp:t_real] for x in pre], db, ds, state_wkv[0],
                                rw_ln_w[0], rw_ln_b[0], rw_rk[0])
    o_rw = jnp.concatenate([o_rw_p, o_rw_s, jnp.zeros((t_pad - t_real, RW_WIDTH), F32)], axis=0)

    h, xn2, te, tg = _merge(x_all, o_nsa, o_rw, mg_all, w_pa[0].astype(BF16), w_pb[0].astype(BF16),
                            w_o[0].astype(BF16), norm2[0], wr_pad, br_pad)

    slot_tok, slot_of, blk_exp, n_used = _routing(te[:, :TOP_K], MOE_BM)
    xs = _rowgather(slot_tok, xn2)
    yb = _experts(blk_exp, n_used, xs, w_up[0].astype(BF16), b_up[0].astype(F32), w_down[0].astype(BF16),
                  b_down[0].astype(F32))
    yk = _rowgather(slot_of.T.reshape(TOP_K * t_pad), yb).reshape(TOP_K, t_pad, D_MODEL)
    y_all = _final(h, yk, tg, norm_f)

    y_prompt = y_all[:tp].reshape(b, s, D_MODEL)
    y_sample = y_all[tp:t_real].reshape(db, ds, D_MODEL)
    kv_shape = (4, NSA_KV_HEADS, HEAD_DIM)
    win_shape = (2, NSA_KV_HEADS, HEAD_DIM)
    kv_prompt = kv_p[:, :, :row_w].reshape((1, b, s) + kv_shape)
    kv_sample = kv_s[:, :, :row_w].reshape((1, db, ds) + kv_shape)
    keep_p = min(WINDOW, s)
    win_prompt = kv_p[:, s - keep_p:, row_w:].reshape((1, b, keep_p) + win_shape)
    win_sample = kw_full[:, nbuf + ds - nbuf:].reshape((1, db, nbuf) + win_shape)
    shift_prompt = rw_p[:, -1][None]
    shift_sample = rw_s[:, -1][None]
    return (y_prompt, y_sample, kv_prompt, kv_sample, win_prompt, win_sample,
            wkv_p[None], wkv_s[None], shift_prompt, shift_sample)
```

```python
import functools
import math

import jax
import jax.numpy as jnp
import numpy as np
from jax import lax
from jax.experimental import pallas as pl
from jax.experimental.pallas import tpu as pltpu

F32 = jnp.float32
BF16 = jnp.bfloat16
HI = lax.Precision.HIGHEST

D_MODEL = 1024
NSA_HEADS = 8
NSA_KV_HEADS = 2
HEAD_DIM = 64
GROUP = NSA_HEADS // NSA_KV_HEADS
NSA_WIDTH = NSA_HEADS * HEAD_DIM
KV_LANES = NSA_KV_HEADS * HEAD_DIM
CMP_BLOCK = 32
CMP_STRIDE = 16
SEL_BLOCK = 64
SEL_TOP = 16
WINDOW = 512
SEL_FORCE = 1e9
NEG_INF = -1e30
PAGE = 128

RW_HEADS = 8
RW_HEAD_DIM = 64
RW_WIDTH = RW_HEADS * RW_HEAD_DIM
RW_DECAY_LORA = 64
RW_AAA_LORA = 64
RW_GATE_LORA = 128
RW_COLS = 3 * RW_WIDTH + RW_DECAY_LORA + RW_AAA_LORA + RW_GATE_LORA
GN_EPS = 64e-5

N_EXPERTS = 32
TOP_K = 4
D_FF = 1024
SWIGLU_LIMIT = 7.0
SWIGLU_ALPHA = 1.702
RMS_EPS = 1e-5

OFF_KV = NSA_WIDTH
OFF_GN = OFF_KV + 6 * KV_LANES
OFF_RW = OFF_GN + 3 * NSA_HEADS
OFF_MG = OFF_RW + RW_COLS
N_PROJ = OFF_MG + 2 * D_MODEL

LANES = 128
GN_PAD = LANES
P_Q = 0
P_KV = P_Q + NSA_WIDTH
P_GN = P_KV + 6 * KV_LANES
P_RW = P_GN + GN_PAD
P_MG = P_RW + RW_COLS
P_END = P_MG + 2 * D_MODEL

VMEM_LIMIT = 52 * 1024 * 1024

TOK_TILE = 256
TOK_ALIGN = 512
MOE_BM = 512
IDX_BLOCKS = 2
KEY_TILE = 128


def _cparams(sem):
    return pltpu.CompilerParams(dimension_semantics=sem, vmem_limit_bytes=VMEM_LIMIT)


def _sigmoid(x):
    return 1.0 / (1.0 + jnp.exp(-x))


def _round_up(a, b):
    return -(-a // b) * b


def _proj_kernel(x_ref, g_ref, w_ref, q_ref, kv_ref, kvb_ref, gn_ref, rw_ref, mg_ref):
    x = x_ref[...]
    ms = jnp.mean(x * x, axis=-1, keepdims=True)
    xn = (x * lax.rsqrt(ms + RMS_EPS) * g_ref[...]).astype(BF16)

    def seg(a, b):
        return jnp.dot(xn, w_ref[:, a:b], preferred_element_type=F32)

    q_ref[...] = (seg(P_Q, P_KV) * (HEAD_DIM ** -0.5)).astype(BF16)
    kv = seg(P_KV, P_GN)
    kv_ref[...] = kv
    kvb_ref[...] = kv.astype(BF16)
    gn_ref[...] = _sigmoid(seg(P_GN, P_RW))
    rw_ref[...] = seg(P_RW, P_MG)
    mg_ref[...] = _sigmoid(seg(P_MG, P_END))


def _proj(x, norm1, w_pad):
    T = x.shape[0]
    tm = TOK_TILE
    row = lambda i: (i, 0)
    const = lambda i: (0, 0)
    outs = (
        jax.ShapeDtypeStruct((T, NSA_WIDTH), BF16),
        jax.ShapeDtypeStruct((T, 6 * KV_LANES), F32),
        jax.ShapeDtypeStruct((T, 6 * KV_LANES), BF16),
        jax.ShapeDtypeStruct((T, GN_PAD), F32),
        jax.ShapeDtypeStruct((T, RW_COLS), F32),
        jax.ShapeDtypeStruct((T, 2 * D_MODEL), F32),
    )
    return pl.pallas_call(
        _proj_kernel,
        grid=(T // tm,),
        in_specs=[pl.BlockSpec((tm, D_MODEL), row), pl.BlockSpec((1, D_MODEL), const),
                  pl.BlockSpec((D_MODEL, P_END), const)],
        out_specs=[pl.BlockSpec((tm, o.shape[1]), row) for o in outs],
        out_shape=outs,
        compiler_params=_cparams(("parallel",)),
        name="proj",
    )(x, norm1.reshape(1, D_MODEL), w_pad)


class _PagedRows:
    def __init__(self, pt_ref, cache_ref, tail_ref, buf_refs, sem_ref, col0):
        self.pt, self.cache, self.tail, self.bufs, self.sem, self.col0 = pt_ref, cache_ref, tail_ref, buf_refs, sem_ref, col0
        self.n_pages = pt_ref.shape[1]

    def _copies(self, src, row0, slot):
        return [pltpu.make_async_copy(src.at[:, pl.ds((self.col0 + i) * KV_LANES, KV_LANES)],
                                      buf.at[slot, pl.ds(row0, PAGE), :], self.sem.at[slot])
                for i, buf in enumerate(self.bufs)]

    def _page(self, b, p, slot):
        return self._copies(self.cache.at[self.pt[b, p]], p * PAGE, slot)

    def _tail(self, b, slot):
        return self._copies(self.tail.at[b], self.n_pages * PAGE, slot)

    def start(self, b, slot):
        def body(p, c):
            for cp in self._page(b, p, slot):
                cp.start()
            return c
        lax.fori_loop(0, self.n_pages, body, 0)
        for cp in self._tail(b, slot):
            cp.start()

    def wait(self, b, slot):
        def body(p, c):
            for cp in self._page(b, p, slot):
                cp.wait()
            return c
        lax.fori_loop(0, self.n_pages, body, 0)
        for cp in self._tail(b, slot):
            cp.wait()

    def fetch(self):
        b = pl.program_id(0)
        slot = b & 1

        @pl.when(b == 0)
        def _():
            self.start(b, slot)

        @pl.when(b + 1 < pl.num_programs(0))
        def _():
            self.start(b + 1, 1 - slot)

        self.wait(b, slot)
        return slot

    def half(self, slot, i):
        return self.bufs[i].at[pl.ds(slot, 1)]


def _gelu_tanh(x):
    return 0.5 * x * (1.0 + jnp.tanh(math.sqrt(2.0 / math.pi) * (x + 0.044715 * (x * x * x))))


def _compress_paged_kernel(pt_ref, cache_ref, tail_ref, mix_ref, pe_ref, w1_ref, w2_ref, kc_ref, vc_ref,
                           kbuf_ref, vbuf_ref, sem_ref):
    rows = _PagedRows(pt_ref, cache_ref, tail_ref, (kbuf_ref, vbuf_ref), sem_ref, 0)
    slot = rows.fetch()
    _compress_kernel(rows.half(slot, 0), rows.half(slot, 1), mix_ref, pe_ref, w1_ref, w2_ref, kc_ref, vc_ref)


def _compress_paged(page_table, cache2d, tail, mix2, pe2, w1bd, w2bd):
    db, n_pages = page_table.shape
    l = (n_pages + 1) * PAGE
    nch = l // CMP_STRIDE
    const3 = lambda i, pt: (0, 0, 0)
    any_spec = pl.BlockSpec(memory_space=pl.ANY)
    out = jax.ShapeDtypeStruct((db, nch, LANES), F32)
    return pl.pallas_call(
        _compress_paged_kernel,
        grid_spec=pltpu.PrefetchScalarGridSpec(
            num_scalar_prefetch=1, grid=(db,),
            in_specs=[any_spec, any_spec,
                      pl.BlockSpec((2, CMP_BLOCK, LANES), const3), pl.BlockSpec((2, CMP_BLOCK, LANES), const3),
                      pl.BlockSpec((2, LANES, LANES), const3), pl.BlockSpec((2, LANES, LANES), const3)],
            out_specs=[pl.BlockSpec((1, nch, LANES), lambda i, pt: (i, 0, 0))] * 2,
            scratch_shapes=[pltpu.VMEM((2, l, KV_LANES), cache2d.dtype)] * 2 + [pltpu.SemaphoreType.DMA((2,))]),
        out_shape=(out, out),
        compiler_params=_cparams(("arbitrary",)),
        name="compress_paged",
    )(page_table, cache2d, tail, mix2, pe2, w1bd, w2bd)


def _compress_kernel(rk_ref, rv_ref, mix_ref, pe_ref, w1_ref, w2_ref, kc_ref, vc_ref):
    nch = kc_ref.shape[1]
    for idx, (r_ref, o_ref) in enumerate(((rk_ref, kc_ref), (rv_ref, vc_ref))):
        mix = mix_ref[idx]
        cs0 = jnp.zeros((nch, LANES), F32)
        cs1 = jnp.zeros((nch, LANES), F32)
        for p in range(CMP_STRIDE):
            xp = r_ref[0, pl.ds(p, nch, stride=CMP_STRIDE), :].astype(F32)
            cs0 = cs0 + xp * mix[p:p + 1]
            cs1 = cs1 + xp * mix[CMP_STRIDE + p:CMP_STRIDE + p + 1]
        c0 = jnp.sum(mix * pe_ref[idx], axis=0, keepdims=True)
        pre = c0 + cs0 + pltpu.roll(cs1, nch - 1, 0)
        h = _gelu_tanh(jnp.dot(pre, w1_ref[idx], precision=HI, preferred_element_type=F32))
        o_ref[0] = jnp.dot(h, w2_ref[idx], precision=HI, preferred_element_type=F32)


def _compress(rows3d, mix2, pe2, w1bd, w2bd):
    b, l, _ = rows3d.shape
    nch = l // CMP_STRIDE
    const3 = lambda i: (0, 0, 0)
    out = jax.ShapeDtypeStruct((b, nch, LANES), F32)
    return pl.pallas_call(
        _compress_kernel,
        grid=(b,),
        in_specs=[pl.BlockSpec((1, l, LANES), lambda i: (i, 0, 0)),
                  pl.BlockSpec((1, l, LANES), lambda i: (i, 0, 1)),
                  pl.BlockSpec((2, CMP_BLOCK, LANES), const3), pl.BlockSpec((2, CMP_BLOCK, LANES), const3),
                  pl.BlockSpec((2, LANES, LANES), const3), pl.BlockSpec((2, LANES, LANES), const3)],
        out_specs=[pl.BlockSpec((1, nch, LANES), lambda i: (i, 0, 0))] * 2,
        out_shape=(out, out),
        compiler_params=_cparams(("parallel",)),
        name="compress",
    )(rows3d, rows3d, mix2, pe2, w1bd, w2bd)


def _masked_softmax(s, mask):
    s = jnp.where(mask, s, NEG_INF)
    m = jnp.max(s, axis=-1, keepdims=True)
    e = jnp.where(mask, jnp.exp(s - m), 0.0)
    return e / jnp.maximum(jnp.sum(e, axis=-1, keepdims=True), 1e-30)


_NT = (((1,), (1,)), ((), ()))


def _flash(qb, k_ref, v_ref, lo, hi, mask_fn, kt_rows):
    r = qb.shape[0]

    def body(kt, carry):
        m, l, acc = carry
        off = pl.multiple_of(kt * kt_rows, kt_rows)
        k = k_ref[0, pl.ds(off, kt_rows), :].astype(BF16)
        v = v_ref[0, pl.ds(off, kt_rows), :].astype(BF16)
        s = lax.dot_general(qb, k, _NT, preferred_element_type=F32)
        mask = mask_fn(kt)
        s = jnp.where(mask, s, NEG_INF)
        m_new = jnp.maximum(m, jnp.max(s, axis=-1, keepdims=True))
        p = jnp.where(mask, jnp.exp(s - m_new), 0.0)
        alpha = jnp.exp(m - m_new)
        l = alpha * l + jnp.sum(p, axis=-1, keepdims=True)
        acc = alpha * acc + jnp.dot(p.astype(BF16), v, preferred_element_type=F32)
        return m_new, l, acc

    init = (jnp.full((r, 1), NEG_INF, F32), jnp.zeros((r, 1), F32), jnp.zeros((r, LANES), F32))
    _, l, acc = lax.fori_loop(lo, hi, body, init)
    return acc / jnp.maximum(l, 1e-30)


def _nsa_kernel(*refs, **static):
    _nsa_body(pl.program_id(1), *refs, **static)


def _nsa_paged_kernel(pt_ref, qbd_ref, gn_ref, kc_ref, vc_ref, cover_ref, cache_ref, tail_ref, kw_ref, vw_ref, e_ref,
                      o_ref, kbuf_ref, vbuf_ref, sem_ref, **static):
    rows = _PagedRows(pt_ref, cache_ref, tail_ref, (kbuf_ref, vbuf_ref), sem_ref, 2)
    slot = rows.fetch()
    _nsa_body(0, qbd_ref, gn_ref, kc_ref, vc_ref, cover_ref, rows.half(slot, 0), rows.half(slot, 1), kw_ref, vw_ref,
              e_ref, o_ref, **static)


def _nsa_body(qi, qbd_ref, gn_ref, kc_ref, vc_ref, cover_ref, ks_ref, vs_ref, kw_ref, vw_ref, e_ref, o_ref,
              *, tq, q_start, kw_start, n_top):
    kt_rows = KEY_TILE
    r = NSA_HEADS * tq
    t0 = q_start + qi * tq
    qb = qbd_ref[0, 0]
    t_row = t0 + (lax.broadcasted_iota(jnp.int32, (r, 1), 0) & (tq - 1))

    kc = kc_ref[0].astype(BF16)
    vc = vc_ref[0].astype(BF16)
    nch = kc.shape[0]
    s_c = lax.dot_general(qb, kc, _NT, preferred_element_type=F32)
    cmp_end = lax.broadcasted_iota(jnp.int32, (1, nch), 1) * CMP_STRIDE + (CMP_BLOCK - 1)
    p_c = _masked_softmax(s_c, cmp_end <= t_row)
    o_c = jnp.dot(p_c.astype(BF16), vc, preferred_element_type=F32)

    ns_pad = cover_ref.shape[1]
    tq_pos = t0 + lax.broadcasted_iota(jnp.int32, (tq, 1), 0)
    cur = tq_pos >> 6
    jb = lax.broadcasted_iota(jnp.int32, (1, ns_pad), 1)
    jbf = jb.astype(F32)
    elig = jb <= cur
    forced = (jb == 0) | (jb == cur) | (jb == cur - 1)
    sels = []
    for g in range(NSA_KV_HEADS):
        base = g * GROUP * tq
        pg = p_c[base:base + tq]
        for n in range(1, GROUP):
            pg = pg + p_c[base + n * tq:base + (n + 1) * tq]
        imp = jnp.dot(pg, cover_ref[...], precision=HI, preferred_element_type=F32)
        score = jnp.where(elig, jnp.where(forced, SEL_FORCE, imp), -SEL_FORCE)
        sel = jnp.zeros((tq, ns_pad), F32)
        for _ in range(n_top):
            m = jnp.max(score, axis=-1, keepdims=True)
            first = jnp.min(jnp.where(score == m, jbf, 1e9), axis=-1, keepdims=True)
            hit = jbf == first
            sel = jnp.where(hit, 1.0, sel)
            score = jnp.where(hit, -3e38, score)
        sels.append(jnp.where(elig, sel, 0.0))
    selrows = jnp.concatenate([sels[0]] * GROUP + [sels[1]] * GROUP, axis=0).astype(BF16)

    def sel_mask(kt):
        blk = kt * (kt_rows // SEL_BLOCK) + (lax.broadcasted_iota(jnp.int32, (ns_pad, kt_rows), 1) >> 6)
        expand = jnp.where(lax.broadcasted_iota(jnp.int32, (ns_pad, kt_rows), 0) == blk, 1.0, 0.0).astype(BF16)
        selx = jnp.dot(selrows, expand, preferred_element_type=F32)
        kpos = kt * kt_rows + lax.broadcasted_iota(jnp.int32, (1, kt_rows), 1)
        return (selx > 0.5) & (kpos <= t_row)

    hi_s = (t0 + tq - 1) // kt_rows + 1
    o_s = _flash(qb, ks_ref, vs_ref, 0, hi_s, sel_mask, kt_rows)

    def win_mask(kt):
        kpos = kw_start + kt * kt_rows + lax.broadcasted_iota(jnp.int32, (1, kt_rows), 1)
        dist = t_row - kpos
        return (dist >= 0) & (dist <= WINDOW)

    n_wt = kw_ref.shape[1] // kt_rows
    lo_w = jnp.maximum(t0 - WINDOW - kw_start, 0) // kt_rows
    hi_w = jnp.minimum((t0 + tq - 1 - kw_start) // kt_rows + 1, n_wt)
    o_w = _flash(qb, kw_ref, vw_ref, lo_w, hi_w, win_mask, kt_rows)

    lane = lax.broadcasted_iota(jnp.int32, (tq, LANES), 1)

    def assemble(o):
        pieces = []
        for j in range(NSA_HEADS // 2):
            a = o[(2 * j) * tq:(2 * j + 1) * tq]
            b = o[(2 * j + 1) * tq:(2 * j + 2) * tq]
            if j < GROUP // 2:
                x, y = a, pltpu.roll(b, HEAD_DIM, 1)
            else:
                x, y = pltpu.roll(a, HEAD_DIM, 1), b
            pieces.append(jnp.where(lane < HEAD_DIM, x, y))
        return jnp.concatenate(pieces, axis=1)

    gn = gn_ref[0]
    out = None
    for br, o in enumerate((o_c, o_s, o_w)):
        gate = jnp.dot(gn, e_ref[br], precision=HI, preferred_element_type=F32)
        term = gate * assemble(o)
        out = term if out is None else out + term
    o_ref[0] = out.astype(o_ref.dtype)


def _cover_matrix(nch, ns_pad):
    ci = np.arange(nch)[:, None] * CMP_STRIDE
    sj = np.arange(ns_pad)[None, :] * SEL_BLOCK
    return jnp.asarray(((ci < sj + SEL_BLOCK) & (ci + CMP_BLOCK > sj)).astype(np.float32))


def _gate_expand():
    e = np.zeros((3, GN_PAD, NSA_WIDTH), np.float32)
    for br in range(3):
        for h in range(NSA_HEADS):
            e[br, br * NSA_HEADS + h, h * HEAD_DIM:(h + 1) * HEAD_DIM] = 1.0
    return jnp.asarray(e)


def _block_diag_q(q, b, sq, tq):
    nqt = sq // tq
    qr = q.reshape(b, nqt, tq, NSA_KV_HEADS, GROUP, HEAD_DIM).transpose(0, 1, 3, 4, 2, 5)
    z = jnp.zeros_like(qr[:, :, 0])
    g0 = jnp.concatenate([qr[:, :, 0], z], axis=-1)
    g1 = jnp.concatenate([z, qr[:, :, 1]], axis=-1)
    return jnp.stack([g0, g1], axis=2).reshape(b, nqt, NSA_HEADS * tq, LANES)


def _nsa(qbd, gn3, kc, vc, ksv, ks_col, kwv, kw_col, *, tq, q_start, kw_start, ns_true):
    b, nqt = qbd.shape[:2]
    sq = nqt * tq
    nch = kc.shape[1]
    lk = ksv.shape[1]
    lw = kwv.shape[1]
    ns_pad = _round_up(max(lk // SEL_BLOCK, LANES), LANES)
    cover = _cover_matrix(nch, ns_pad)
    n_top = min(SEL_TOP, ns_true)
    kern = functools.partial(_nsa_kernel, tq=tq, q_start=q_start, kw_start=kw_start, n_top=n_top)
    per_b = lambda i, j: (i, 0, 0)
    return pl.pallas_call(
        kern,
        grid=(b, nqt),
        in_specs=[pl.BlockSpec((1, 1, NSA_HEADS * tq, LANES), lambda i, j: (i, j, 0, 0)),
                  pl.BlockSpec((1, tq, GN_PAD), lambda i, j: (i, j, 0)),
                  pl.BlockSpec((1, nch, LANES), per_b), pl.BlockSpec((1, nch, LANES), per_b),
                  pl.BlockSpec((nch, ns_pad), lambda i, j: (0, 0)),
                  pl.BlockSpec((1, lk, LANES), lambda i, j: (i, 0, ks_col)),
                  pl.BlockSpec((1, lk, LANES), lambda i, j: (i, 0, ks_col + 1)),
                  pl.BlockSpec((1, lw, LANES), lambda i, j: (i, 0, kw_col)),
                  pl.BlockSpec((1, lw, LANES), lambda i, j: (i, 0, kw_col + 1)),
                  pl.BlockSpec((3, GN_PAD, NSA_WIDTH), lambda i, j: (0, 0, 0))],
        out_specs=pl.BlockSpec((1, tq, NSA_WIDTH), lambda i, j: (i, j, 0)),
        out_shape=jax.ShapeDtypeStruct((b, sq, NSA_WIDTH), BF16),
        compiler_params=_cparams(("parallel", "arbitrary")),
        name="nsa",
    )(qbd, gn3, kc, vc, cover, ksv, ksv, kwv, kwv, _gate_expand())


def _nsa_paged(qbd, gn3, kc, vc, page_table, cache2d, tail, kwv, *, tq, q_start, kw_start, ns_true):
    db, nqt = qbd.shape[:2]
    assert nqt == 1
    n_pages = page_table.shape[1]
    nch = kc.shape[1]
    lk = (n_pages + 1) * PAGE
    lw = kwv.shape[1]
    ns_pad = _round_up(max(lk // SEL_BLOCK, LANES), LANES)
    cover = _cover_matrix(nch, ns_pad)
    n_top = min(SEL_TOP, ns_true)
    kern = functools.partial(_nsa_paged_kernel, tq=tq, q_start=q_start, kw_start=kw_start, n_top=n_top)
    per_b = lambda i, pt: (i, 0, 0)
    any_spec = pl.BlockSpec(memory_space=pl.ANY)
    return pl.pallas_call(
        kern,
        grid_spec=pltpu.PrefetchScalarGridSpec(
            num_scalar_prefetch=1, grid=(db,),
            in_specs=[pl.BlockSpec((1, 1, NSA_HEADS * tq, LANES), lambda i, pt: (i, 0, 0, 0)),
                      pl.BlockSpec((1, tq, GN_PAD), per_b),
                      pl.BlockSpec((1, nch, LANES), per_b), pl.BlockSpec((1, nch, LANES), per_b),
                      pl.BlockSpec((nch, ns_pad), lambda i, pt: (0, 0)),
                      any_spec, any_spec,
                      pl.BlockSpec((1, lw, LANES), lambda i, pt: (i, 0, 0)),
                      pl.BlockSpec((1, lw, LANES), lambda i, pt: (i, 0, 1)),
                      pl.BlockSpec((3, GN_PAD, NSA_WIDTH), lambda i, pt: (0, 0, 0))],
            out_specs=pl.BlockSpec((1, tq, NSA_WIDTH), per_b),
            scratch_shapes=[pltpu.VMEM((2, lk, KV_LANES), cache2d.dtype)] * 2 + [pltpu.SemaphoreType.DMA((2,))]),
        out_shape=jax.ShapeDtypeStruct((db, tq, NSA_WIDTH), BF16),
        compiler_params=_cparams(("arbitrary",)),
        name="nsa_paged",
    )(page_table, qbd, gn3, kc, vc, cover, cache2d, tail, kwv, kwv, _gate_expand())


def _rwkv_pre_kernel(c_ref, p_ref, mu_ref, w0_ref, a0_ref, kkw_ref, ka_ref, wl_ref, g2_ref,
                     r_ref, k_ref, v_ref, kk_ref, a_ref, d_ref, g_ref):
    cols = c_ref[...]
    xs = cols + mu_ref[...] * (p_ref[...] - cols)
    o1, o2, o3 = RW_WIDTH, 2 * RW_WIDTH, 3 * RW_WIDTH
    o5 = o3 + RW_DECAY_LORA + RW_AAA_LORA
    k = xs[:, o1:o2]
    wa = xs[:, o3:o5]
    lane = lax.broadcasted_iota(jnp.int32, wa.shape, 1)
    wa = jnp.where(lane < RW_DECAY_LORA, jnp.tanh(wa), wa)
    lo = jnp.dot(wa.astype(BF16), wl_ref[...], preferred_element_type=F32)
    z = -(w0_ref[...] + lo[:, :RW_WIDTH])
    softplus = jnp.maximum(z, 0.0) + jnp.log(1.0 + jnp.exp(-jnp.abs(z)))
    w = -softplus - 0.5
    a = _sigmoid(a0_ref[...] + lo[:, RW_WIDTH:])
    r_ref[...] = xs[:, :o1]
    k_ref[...] = k * (1.0 + (a - 1.0) * ka_ref[...])
    v_ref[...] = xs[:, o2:o3]
    kk_ref[...] = k * kkw_ref[...]
    a_ref[...] = a
    d_ref[...] = jnp.exp(-jnp.exp(w))
    g_ref[...] = jnp.dot(_sigmoid(xs[:, o5:]).astype(BF16), g2_ref[...], preferred_element_type=F32)


def _rwkv_pre(cols, prev, mu, w0, a0, kkw, ka, wlora, g2):
    t = cols.shape[0]
    tm = TOK_TILE
    row = lambda i: (i, 0)
    const = lambda i: (0, 0)
    vec = pl.BlockSpec((1, RW_WIDTH), const)
    out = jax.ShapeDtypeStruct((t, RW_WIDTH), F32)
    return pl.pallas_call(
        _rwkv_pre_kernel,
        grid=(t // tm,),
        in_specs=[pl.BlockSpec((tm, RW_COLS), row), pl.BlockSpec((tm, RW_COLS), row),
                  pl.BlockSpec((1, RW_COLS), const), vec, vec, vec, vec,
                  pl.BlockSpec((RW_DECAY_LORA + RW_AAA_LORA, 2 * RW_WIDTH), const),
                  pl.BlockSpec((RW_GATE_LORA, RW_WIDTH), const)],
        out_specs=[pl.BlockSpec((tm, RW_WIDTH), row)] * 7,
        out_shape=(out,) * 7,
        compiler_params=_cparams(("parallel",)),
        name="rwkv_pre",
    )(cols, prev, mu, w0, a0, kkw, ka, wlora, g2)


def _rwkv_scan_kernel(r_ref, k_ref, v_ref, kk_ref, a_ref, d_ref, g_ref, s0_ref, lnw_ref, lnb_ref, rk_ref,
                      o_ref, sf_ref, st_ref, kkn_ref, b_ref):
    n = RW_HEAD_DIM
    tb = r_ref.shape[0]
    ti = pl.program_id(1)

    @pl.when(ti == 0)
    def _():
        st_ref[...] = s0_ref[...]

    def step(t, c):
        kk = kk_ref[t]
        nrm = jnp.sqrt(jnp.sum(kk * kk, axis=0, keepdims=True))
        kkn = kk / jnp.maximum(nrm, 1e-12)
        kkn_ref[...] = kkn
        b_ref[...] = kkn * a_ref[t]
        vv = v_ref[t]
        sa = jnp.zeros((n, LANES), F32)
        for i in range(n):
            sa = sa - st_ref[i] * kkn_ref[pl.ds(i, 1), :]
        y = jnp.zeros((n, LANES), F32)
        for i in range(n):
            s_new = (st_ref[i] * d_ref[t, pl.ds(i, 1), :] + sa * b_ref[pl.ds(i, 1), :]
                     + vv * k_ref[t, pl.ds(i, 1), :])
            st_ref[i] = s_new
            y = y + s_new * r_ref[t, pl.ds(i, 1), :]
        mu = jnp.mean(y, axis=0, keepdims=True)
        yc = y - mu
        var = jnp.mean(yc * yc, axis=0, keepdims=True)
        yn = yc * lax.rsqrt(var + GN_EPS) * lnw_ref[...] + lnb_ref[...]
        bonus = jnp.sum(r_ref[t] * k_ref[t] * rk_ref[...], axis=0, keepdims=True) * vv
        o_ref[t] = (yn + bonus) * g_ref[t]
        return c

    lax.fori_loop(0, tb, step, 0)

    @pl.when(ti == pl.num_programs(1) - 1)
    def _():
        sf_ref[...] = st_ref[...]


def _rwkv_scan(seqs, s0, lnw, lnb, rk, tb):
    s, n, nl = seqs[0].shape
    blk = pl.BlockSpec((tb, n, LANES), lambda j, i: (i, 0, j))
    st = pl.BlockSpec((n, n, LANES), lambda j, i: (0, 0, j))
    vec = pl.BlockSpec((n, LANES), lambda j, i: (0, j))
    return pl.pallas_call(
        _rwkv_scan_kernel,
        grid=(nl // LANES, s // tb),
        in_specs=[blk] * 7 + [st, vec, vec, vec],
        out_specs=[blk, st],
        out_shape=(jax.ShapeDtypeStruct((s, n, nl), F32), jax.ShapeDtypeStruct((n, n, nl), F32)),
        scratch_shapes=[pltpu.VMEM((n, n, LANES), F32), pltpu.VMEM((n, LANES), F32), pltpu.VMEM((n, LANES), F32)],
        compiler_params=_cparams(("parallel", "arbitrary")),
        name="rwkv_scan",
    )(*seqs, s0, lnw, lnb, rk)


def _to_scan_layout(x, b, s):
    return x.reshape(b, s, RW_HEADS, RW_HEAD_DIM).transpose(1, 3, 0, 2).reshape(s, RW_HEAD_DIM, b * RW_HEADS)


def _from_scan_layout(y, b, s):
    return y.reshape(s, RW_HEAD_DIM, b, RW_HEADS).transpose(2, 0, 3, 1).reshape(b * s, RW_WIDTH)


def _rwkv_group(pre, b, s, wkv0, lnw, lnb, rk):
    nl = b * RW_HEADS
    nlp = _round_up(nl, LANES)

    def padl(x):
        return x if nlp == nl else jnp.pad(x, [(0, 0)] * (x.ndim - 1) + [(0, nlp - nl)])

    seqs = [padl(_to_scan_layout(x, b, s)) for x in pre]
    s0 = padl(wkv0.astype(F32).transpose(3, 2, 0, 1).reshape(RW_HEAD_DIM, RW_HEAD_DIM, nl))
    tile = lambda p: padl(jnp.tile(p.reshape(RW_HEADS, RW_HEAD_DIM).T, (1, b)))
    tb = math.gcd(s, 32)
    o, sf = _rwkv_scan(seqs, s0, tile(lnw), tile(lnb), tile(rk), tb)
    o = _from_scan_layout(o[:, :, :nl], b, s)
    sf = sf[:, :, :nl].reshape(RW_HEAD_DIM, RW_HEAD_DIM, b, RW_HEADS).transpose(2, 3, 1, 0)
    return o, sf


def _merge_kernel(x_ref, on_ref, or_ref, mg_ref, wpa_ref, wpb_ref, wo_ref, n2_ref, wr_ref, br_ref,
                  h_ref, xn_ref, te_ref, tg_ref):
    pa = jnp.dot(on_ref[...], wpa_ref[...], preferred_element_type=F32)
    pb = jnp.dot(or_ref[...].astype(BF16), wpb_ref[...], preferred_element_type=F32)
    merged = mg_ref[:, :D_MODEL] * pa + mg_ref[:, D_MODEL:] * pb
    h = x_ref[...] + jnp.dot(merged.astype(BF16), wo_ref[...], preferred_element_type=F32)
    h_ref[...] = h
    ms = jnp.mean(h * h, axis=-1, keepdims=True)
    xn = h * lax.rsqrt(ms + RMS_EPS) * n2_ref[...]
    xn_ref[...] = xn
    logits = jnp.dot(xn, wr_ref[...], precision=HI, preferred_element_type=F32) + br_ref[...]
    lane = lax.broadcasted_iota(jnp.int32, logits.shape, 1)
    lanef = lane.astype(F32)
    te = jnp.zeros(logits.shape, F32)
    tv = jnp.full(logits.shape, NEG_INF, F32)
    for kx in range(TOP_K):
        m = jnp.max(logits, axis=-1, keepdims=True)
        first = jnp.min(jnp.where(logits == m, lanef, 1e9), axis=-1, keepdims=True)
        te = jnp.where(lane == kx, first, te)
        tv = jnp.where(lane == kx, m, tv)
        logits = jnp.where(lanef == first, -3e38, logits)
    e = jnp.where(lane < TOP_K, jnp.exp(tv - jnp.max(tv, axis=-1, keepdims=True)), 0.0)
    te_ref[...] = te.astype(jnp.int32)
    tg_ref[...] = e / jnp.sum(e, axis=-1, keepdims=True)


def _merge(x, o_nsa, o_rw, mg, wpa, wpb, wo, norm2, wr_pad, br_pad):
    t = x.shape[0]
    tm = TOK_TILE
    row = lambda i: (i, 0)
    const = lambda i: (0, 0)
    outs = (jax.ShapeDtypeStruct((t, D_MODEL), F32), jax.ShapeDtypeStruct((t, D_MODEL), F32),
            jax.ShapeDtypeStruct((t, LANES), jnp.int32), jax.ShapeDtypeStruct((t, LANES), F32))
    return pl.pallas_call(
        _merge_kernel,
        grid=(t // tm,),
        in_specs=[pl.BlockSpec((tm, D_MODEL), row), pl.BlockSpec((tm, NSA_WIDTH), row),
                  pl.BlockSpec((tm, RW_WIDTH), row), pl.BlockSpec((tm, 2 * D_MODEL), row),
                  pl.BlockSpec((NSA_WIDTH, D_MODEL), const), pl.BlockSpec((RW_WIDTH, D_MODEL), const),
                  pl.BlockSpec((D_MODEL, D_MODEL), const), pl.BlockSpec((1, D_MODEL), const),
                  pl.BlockSpec((D_MODEL, LANES), const), pl.BlockSpec((1, LANES), const)],
        out_specs=[pl.BlockSpec((tm, o.shape[1]), row) for o in outs],
        out_shape=outs,
        compiler_params=_cparams(("parallel",)),
        name="merge",
    )(x, o_nsa, o_rw, mg, wpa, wpb, wo, norm2.reshape(1, D_MODEL), wr_pad, br_pad)


class _GatheredRows:
    def __init__(self, src_ref, buf_ref, sem_ref):
        self.src, self.buf, self.sem = src_ref, buf_ref, sem_ref
        self.n = buf_ref.shape[1]

    def start(self, idx_ref, off, slot):
        def body(j, c):
            pltpu.make_async_copy(self.src.at[pl.ds(idx_ref[off + j], 1)], self.buf.at[slot, pl.ds(j, 1)],
                                  self.sem.at[slot]).start()
            return c
        lax.fori_loop(0, self.n, body, 0, unroll=8)

    def wait(self, slot):
        pltpu.make_async_copy(self.src.at[pl.ds(0, self.n)], self.buf.at[slot], self.sem.at[slot]).wait()


def _expert_kernel(be_ref, nu_ref, idx_ref, idxn_ref, xn_ref, wu_ref, bu_ref, wd_ref, bd_ref, o_ref, xbuf_ref, sem_ref):
    i = pl.program_id(0)
    n_used = nu_ref[0]
    slot = i & 1
    bm = xbuf_ref.shape[1]
    rows = _GatheredRows(xn_ref, xbuf_ref, sem_ref)

    @pl.when(i == 0)
    def _():
        rows.start(idx_ref, 0, slot)

    @pl.when(i + 1 < n_used)
    def _():
        rows.start(idxn_ref, (1 - slot) * bm, 1 - slot)

    @pl.when(i < n_used)
    def _():
        rows.wait(slot)
        x = xbuf_ref[slot].astype(BF16)
        hu = jnp.dot(x, wu_ref[0], preferred_element_type=F32) + bu_ref[0]
        gt = jnp.minimum(hu[:, :D_FF], SWIGLU_LIMIT)
        up = jnp.clip(hu[:, D_FF:], -SWIGLU_LIMIT, SWIGLU_LIMIT)
        hh = (up + 1.0) * gt * _sigmoid(SWIGLU_ALPHA * gt)
        o_ref[...] = jnp.dot(hh.astype(BF16), wd_ref[0], preferred_element_type=F32) + bd_ref[0]

    @pl.when(i >= n_used)
    def _():
        o_ref[...] = jnp.zeros(o_ref.shape, o_ref.dtype)


def _experts(blk_exp, n_used, slot_tok, xn, w_up, b_up, w_down, b_down):
    p = slot_tok.shape[0]
    bm = MOE_BM
    nb = p // bm
    assert nb % 2 == 0
    wmap = lambda i, be, nu: (be[i], 0, 0)
    return pl.pallas_call(
        _expert_kernel,
        grid_spec=pltpu.PrefetchScalarGridSpec(
            num_scalar_prefetch=2, grid=(nb,),
            in_specs=[pl.BlockSpec((2 * bm,), lambda i, be, nu: (i // 2,), memory_space=pltpu.SMEM),
                      pl.BlockSpec((2 * bm,), lambda i, be, nu: (jnp.minimum(i + 1, nb - 1) // 2,),
                                   memory_space=pltpu.SMEM),
                      pl.BlockSpec(memory_space=pl.ANY),
                      pl.BlockSpec((1, D_MODEL, 2 * D_FF), wmap), pl.BlockSpec((1, 1, 2 * D_FF), wmap),
                      pl.BlockSpec((1, D_FF, D_MODEL), wmap), pl.BlockSpec((1, 1, D_MODEL), wmap)],
            out_specs=pl.BlockSpec((bm, D_MODEL), lambda i, be, nu: (i, 0)),
            scratch_shapes=[pltpu.VMEM((2, bm, D_MODEL), xn.dtype), pltpu.SemaphoreType.DMA((2,))]),
        out_shape=jax.ShapeDtypeStruct((p, D_MODEL), F32),
        compiler_params=_cparams(("arbitrary",)),
        name="experts",
    )(blk_exp, n_used, slot_tok, slot_tok, xn, w_up, b_up.reshape(N_EXPERTS, 1, 2 * D_FF), w_down,
      b_down.reshape(N_EXPERTS, 1, D_MODEL))


def _final_kernel(idx_ref, idxn_ref, h_ref, yb_ref, tg_ref, nf_ref, o_ref, ybuf_ref, sem_ref):
    i = pl.program_id(0)
    slot = i & 1
    tm = h_ref.shape[0]
    rows = _GatheredRows(yb_ref, ybuf_ref, sem_ref)

    @pl.when(i == 0)
    def _():
        rows.start(idx_ref, 0, slot)

    @pl.when(i + 1 < pl.num_programs(0))
    def _():
        rows.start(idxn_ref, 0, 1 - slot)

    rows.wait(slot)
    tg = tg_ref[...]
    f = tg[:, 0:1] * ybuf_ref[slot, pl.ds(0, tm), :]
    for kx in range(1, TOP_K):
        f = f + tg[:, kx:kx + 1] * ybuf_ref[slot, pl.ds(kx * tm, tm), :]
    y = h_ref[...] + f
    ms = jnp.mean(y * y, axis=-1, keepdims=True)
    o_ref[...] = y * lax.rsqrt(ms + RMS_EPS) * nf_ref[...]


def _final(h, yb, slot_of, tg, norm_f):
    t = h.shape[0]
    tm = TOK_TILE
    nt = t // tm
    idx = slot_of.reshape(nt, tm, TOP_K).transpose(0, 2, 1).reshape(nt * TOP_K * tm)
    row = lambda i: (i, 0)
    return pl.pallas_call(
        _final_kernel,
        grid=(nt,),
        in_specs=[pl.BlockSpec((TOP_K * tm,), lambda i: (i,), memory_space=pltpu.SMEM),
                  pl.BlockSpec((TOP_K * tm,), lambda i: (jnp.minimum(i + 1, nt - 1),), memory_space=pltpu.SMEM),
                  pl.BlockSpec((tm, D_MODEL), row), pl.BlockSpec(memory_space=pl.ANY),
                  pl.BlockSpec((tm, LANES), row), pl.BlockSpec((1, D_MODEL), lambda i: (0, 0))],
        out_specs=pl.BlockSpec((tm, D_MODEL), row),
        out_shape=jax.ShapeDtypeStruct((t, D_MODEL), F32),
        scratch_shapes=[pltpu.VMEM((2, TOP_K * tm, D_MODEL), yb.dtype), pltpu.SemaphoreType.DMA((2,))],
        compiler_params=_cparams(("arbitrary",)),
        name="final",
    )(idx, idx, h, yb, tg, norm_f.reshape(1, D_MODEL))


def _routing(top_e, bm):
    t = top_e.shape[0]
    a = t * TOP_K
    e_flat = top_e.reshape(a)
    order = jnp.argsort(e_flat).astype(jnp.int32)
    e_sorted = e_flat[order]
    counts = jnp.bincount(e_flat, length=N_EXPERTS).astype(jnp.int32)
    starts = jnp.cumsum(counts) - counts
    padded = (counts + bm - 1) // bm * bm
    pends = jnp.cumsum(padded)
    pstarts = pends - padded
    slot_sorted = (pstarts[e_sorted] + jnp.arange(a, dtype=jnp.int32) - starts[e_sorted]).astype(jnp.int32)
    n_blocks = _round_up(-(-a // bm) + N_EXPERTS, IDX_BLOCKS)
    p = n_blocks * bm
    slot_tok = jnp.zeros((p,), jnp.int32).at[slot_sorted].set(order // TOP_K)
    slot_of = jnp.zeros((a,), jnp.int32).at[order].set(slot_sorted).reshape(t, TOP_K)
    blk_exp = jnp.minimum(jnp.sum(jnp.arange(n_blocks)[:, None] * bm >= pends[None, :], axis=1), N_EXPERTS - 1)
    n_used = (pends[-1:] // bm).astype(jnp.int32)
    return slot_tok, slot_of, blk_exp.astype(jnp.int32), n_used


def kernel(x_prompt, x_sample, cache_kv, cache_win, state_wkv, state_shift, page_table, norm1, w_in, cmp_pe, cmp_mix, cmp_w1, cmp_w2, rw_mu, rw_w0, rw_w2, rw_a0, rw_a2, rw_g2, rw_kk, rw_ka, rw_rk, rw_ln_w, rw_ln_b, w_pa, w_pb, w_o, norm2, w_router, b_router, w_up, b_up, w_down, b_down, norm_f):
    assert w_in.shape[0] == 1, "single-layer trunk"
    b, s, _ = x_prompt.shape
    db, ds, _ = x_sample.shape
    n_pages = page_table.shape[1]
    assert cache_kv.shape[2] == PAGE
    past = n_pages * PAGE
    nbuf = cache_win.shape[2]
    tp, ts = b * s, db * ds
    t_real = tp + ts
    t_pad = _round_up(t_real, TOK_ALIGN)

    wi = w_in[0]
    w_pad = jnp.concatenate(
        [wi[:, :OFF_GN], jnp.pad(wi[:, OFF_GN:OFF_RW], ((0, 0), (0, GN_PAD - 3 * NSA_HEADS))), wi[:, OFF_RW:]],
        axis=1).astype(BF16)
    tile_g = lambda p: jnp.tile(p, (1, 1, NSA_KV_HEADS))
    mix2, pe2 = tile_g(cmp_mix[0].astype(F32)), tile_g(cmp_pe[0].astype(F32))
    bd = lambda w: jnp.stack([jnp.kron(jnp.eye(NSA_KV_HEADS, dtype=F32), w[i].astype(F32)) for i in range(2)])
    w1bd, w2bd = bd(cmp_w1[0]), bd(cmp_w2[0])
    zl = jnp.zeros((RW_DECAY_LORA, RW_WIDTH), F32)
    wlora = jnp.concatenate([jnp.concatenate([rw_w2[0], zl], axis=1),
                             jnp.concatenate([zl, rw_a2[0]], axis=1)], axis=0).astype(BF16)
    vec = lambda p: p[0].reshape(1, -1).astype(F32)
    wr_pad = jnp.pad(w_router[0].astype(F32), ((0, 0), (0, LANES - N_EXPERTS)))
    br_pad = jnp.concatenate([b_router[0].astype(F32), jnp.full((LANES - N_EXPERTS,), NEG_INF, F32)]).reshape(1, LANES)

    x_all = jnp.concatenate([x_prompt.reshape(tp, D_MODEL), x_sample.reshape(ts, D_MODEL)], axis=0)
    x_all = jnp.pad(x_all, ((0, t_pad - t_real), (0, 0)))
    q_all, kv_all, kvb_all, gn_all, rw_all, mg_all = _proj(x_all, norm1[0], w_pad)

    kv_p = kv_all[:tp].reshape(b, s, 6 * KV_LANES)
    kv_s = kv_all[tp:t_real].reshape(db, ds, 6 * KV_LANES)
    kvb_p = kvb_all[:tp].reshape(b, s, 6 * KV_LANES)

    assert s % KEY_TILE == 0 and ds <= PAGE
    tq_p = math.gcd(s, 128)
    kc_p, vc_p = _compress(kv_p, mix2, pe2, w1bd, w2bd)
    o_nsa_p = _nsa(_block_diag_q(q_all[:tp], b, s, tq_p), gn_all[:tp].reshape(b, s, GN_PAD), kc_p, vc_p,
                   kvb_p, 2, kvb_p, 4, tq=tq_p, q_start=0, kw_start=0, ns_true=s // SEL_BLOCK)

    row_w = 4 * KV_LANES
    tail = jnp.pad(kv_s[:, :, :row_w], ((0, 0), (0, PAGE - ds), (0, 0)))
    cache2d = cache_kv[0].reshape(-1, PAGE, row_w)
    kc_s, vc_s = _compress_paged(page_table, cache2d, tail, mix2, pe2, w1bd, w2bd)
    kw_full = jnp.concatenate([cache_win[0].reshape(db, nbuf, 2 * KV_LANES), kv_s[:, :, row_w:]], axis=1)
    lw = _round_up(nbuf + ds, KEY_TILE)
    kw_pad = jnp.pad(kw_full, ((0, 0), (0, lw - (nbuf + ds)), (0, 0)))
    lp = _round_up(past + ds, SEL_BLOCK)
    o_nsa_s = _nsa_paged(_block_diag_q(q_all[tp:t_real], db, ds, ds), gn_all[tp:t_real].reshape(db, ds, GN_PAD),
                         kc_s, vc_s, page_table, cache2d, tail, kw_pad,
                         tq=ds, q_start=past, kw_start=past - nbuf, ns_true=lp // SEL_BLOCK)
    o_nsa = jnp.concatenate([o_nsa_p.reshape(tp, NSA_WIDTH), o_nsa_s.reshape(ts, NSA_WIDTH),
                             jnp.zeros((t_pad - t_real, NSA_WIDTH), BF16)], axis=0)

    rw_p = rw_all[:tp].reshape(b, s, RW_COLS)
    rw_s = rw_all[tp:t_real].reshape(db, ds, RW_COLS)
    prev_p = jnp.concatenate([jnp.zeros((b, 1, RW_COLS), F32), rw_p[:, :-1]], axis=1)
    prev_s = jnp.concatenate([state_shift[0][:, None].astype(F32), rw_s[:, :-1]], axis=1)
    prev_all = jnp.concatenate([prev_p.reshape(tp, RW_COLS), prev_s.reshape(ts, RW_COLS),
                                jnp.zeros((t_pad - t_real, RW_COLS), F32)], axis=0)
    pre = _rwkv_pre(rw_all, prev_all, vec(rw_mu), vec(rw_w0), vec(rw_a0), vec(rw_kk), vec(rw_ka), wlora,
                    rw_g2[0].astype(BF16))
    o_rw_p, wkv_p = _rwkv_group([x[:tp] for x in pre], b, s,
                                jnp.zeros((b, RW_HEADS, RW_HEAD_DIM, RW_HEAD_DIM), F32),
                                rw_ln_w[0], rw_ln_b[0], rw_rk[0])
    o_rw_s, wkv_s = _rwkv_group([x[tp:t_real] for x in pre], db, ds, state_wkv[0],
                                rw_ln_w[0], rw_ln_b[0], rw_rk[0])
    o_rw = jnp.concatenate([o_rw_p, o_rw_s, jnp.zeros((t_pad - t_real, RW_WIDTH), F32)], axis=0)

    h, xn2, te, tg = _merge(x_all, o_nsa, o_rw, mg_all, w_pa[0].astype(BF16), w_pb[0].astype(BF16),
                            w_o[0].astype(BF16), norm2[0], wr_pad, br_pad)

    slot_tok, slot_of, blk_exp, n_used = _routing(te[:, :TOP_K], MOE_BM)
    yb = _experts(blk_exp, n_used, slot_tok, xn2, w_up[0].astype(BF16), b_up[0].astype(F32), w_down[0].astype(BF16),
                  b_down[0].astype(F32))
    y_all = _final(h, yb, slot_of, tg, norm_f)

    y_prompt = y_all[:tp].reshape(b, s, D_MODEL)
    y_sample = y_all[tp:t_real].reshape(db, ds, D_MODEL)
    kv_shape = (4, NSA_KV_HEADS, HEAD_DIM)
    win_shape = (2, NSA_KV_HEADS, HEAD_DIM)
    kv_prompt = kv_p[:, :, :row_w].reshape((1, b, s) + kv_shape)
    kv_sample = kv_s[:, :, :row_w].reshape((1, db, ds) + kv_shape)
    keep_p = min(WINDOW, s)
    win_prompt = kv_p[:, s - keep_p:, row_w:].reshape((1, b, keep_p) + win_shape)
    win_sample = kw_full[:, nbuf + ds - nbuf:].reshape((1, db, nbuf) + win_shape)
    shift_prompt = rw_p[:, -1][None]
    shift_sample = rw_s[:, -1][None]
    return (y_prompt, y_sample, kv_prompt, kv_sample, win_prompt, win_sample,
            wkv_p[None], wkv_s[None], shift_prompt, shift_sample)
```

```python
import functools
import math

import jax
import jax.numpy as jnp
import numpy as np
from jax import lax
from jax.experimental import pallas as pl
from jax.experimental.pallas import tpu as pltpu

F32 = jnp.float32
BF16 = jnp.bfloat16
HI = lax.Precision.HIGHEST

D_MODEL = 1024
NSA_HEADS = 8
NSA_KV_HEADS = 2
HEAD_DIM = 64
GROUP = NSA_HEADS // NSA_KV_HEADS
NSA_WIDTH = NSA_HEADS * HEAD_DIM
KV_LANES = NSA_KV_HEADS * HEAD_DIM
CMP_BLOCK = 32
CMP_STRIDE = 16
SEL_BLOCK = 64
SEL_TOP = 16
WINDOW = 512
SEL_FORCE = 1e9
NEG_INF = -1e30
SEL_MASK_BIG = 2.0 ** 100
PAGE = 128

RW_HEADS = 8
RW_HEAD_DIM = 64
RW_WIDTH = RW_HEADS * RW_HEAD_DIM
RW_DECAY_LORA = 64
RW_AAA_LORA = 64
RW_GATE_LORA = 128
RW_COLS = 3 * RW_WIDTH + RW_DECAY_LORA + RW_AAA_LORA + RW_GATE_LORA
GN_EPS = 64e-5

N_EXPERTS = 32
TOP_K = 4
D_FF = 1024
SWIGLU_LIMIT = 7.0
SWIGLU_ALPHA = 1.702
RMS_EPS = 1e-5

OFF_KV = NSA_WIDTH
OFF_GN = OFF_KV + 6 * KV_LANES
OFF_RW = OFF_GN + 3 * NSA_HEADS
OFF_MG = OFF_RW + RW_COLS
N_PROJ = OFF_MG + 2 * D_MODEL

LANES = 128
GN_PAD = LANES
P_Q = 0
P_KV = P_Q + NSA_WIDTH
P_GN = P_KV + 6 * KV_LANES
P_RW = P_GN + GN_PAD
P_MG = P_RW + RW_COLS
P_END = P_MG + 2 * D_MODEL

VMEM_LIMIT = 52 * 1024 * 1024

TOK_TILE = 256
TOK_ALIGN = 512
MOE_BM = 512
IDX_BLOCKS = 2
KEY_TILE = 128
NSA_TQ = 128


def _cparams(sem):
    return pltpu.CompilerParams(dimension_semantics=sem, vmem_limit_bytes=VMEM_LIMIT)


def _sigmoid(x):
    return 1.0 / (1.0 + jnp.exp(-x))


def _round_up(a, b):
    return -(-a // b) * b


def _proj_kernel(x_ref, g_ref, w_ref, q_ref, kv_ref, kvb_ref, gn_ref, rw_ref, mg_ref):
    x = x_ref[...]
    ms = jnp.mean(x * x, axis=-1, keepdims=True)
    xn = (x * lax.rsqrt(ms + RMS_EPS) * g_ref[...]).astype(BF16)

    def seg(a, b):
        return jnp.dot(xn, w_ref[:, a:b], preferred_element_type=F32)

    q_ref[...] = (seg(P_Q, P_KV) * (HEAD_DIM ** -0.5)).astype(BF16)
    kv = seg(P_KV, P_GN)
    kv_ref[...] = kv
    kvb_ref[...] = kv.astype(BF16)
    gn_ref[...] = _sigmoid(seg(P_GN, P_RW))
    rw_ref[...] = seg(P_RW, P_MG)
    mg_ref[...] = _sigmoid(seg(P_MG, P_END))


def _proj(x, norm1, w_pad):
    T = x.shape[0]
    tm = TOK_TILE
    row = lambda i: (i, 0)
    const = lambda i: (0, 0)
    outs = (
        jax.ShapeDtypeStruct((T, NSA_WIDTH), BF16),
        jax.ShapeDtypeStruct((T, 6 * KV_LANES), F32),
        jax.ShapeDtypeStruct((T, 6 * KV_LANES), BF16),
        jax.ShapeDtypeStruct((T, GN_PAD), F32),
        jax.ShapeDtypeStruct((T, RW_COLS), F32),
        jax.ShapeDtypeStruct((T, 2 * D_MODEL), F32),
    )
    return pl.pallas_call(
        _proj_kernel,
        grid=(T // tm,),
        in_specs=[pl.BlockSpec((tm, D_MODEL), row), pl.BlockSpec((1, D_MODEL), const),
                  pl.BlockSpec((D_MODEL, P_END), const)],
        out_specs=[pl.BlockSpec((tm, o.shape[1]), row) for o in outs],
        out_shape=outs,
        compiler_params=_cparams(("parallel",)),
        name="proj",
    )(x, norm1.reshape(1, D_MODEL), w_pad)


class _PagedRows:
    def __init__(self, pt_ref, cache_ref, tail_ref, buf_refs, sem_ref, col0):
        self.pt, self.cache, self.tail, self.bufs, self.sem, self.col0 = pt_ref, cache_ref, tail_ref, buf_refs, sem_ref, col0
        self.n_pages = pt_ref.shape[1]

    def _copies(self, src, row0, slot):
        return [pltpu.make_async_copy(src.at[:, pl.ds((self.col0 + i) * KV_LANES, KV_LANES)],
                                      buf.at[slot, pl.ds(row0, PAGE), :], self.sem.at[slot])
                for i, buf in enumerate(self.bufs)]

    def _page(self, b, p, slot):
        return self._copies(self.cache.at[self.pt[b, p]], p * PAGE, slot)

    def _tail(self, b, slot):
        return self._copies(self.tail.at[b], self.n_pages * PAGE, slot)

    def start(self, b, slot):
        def body(p, c):
            for cp in self._page(b, p, slot):
                cp.start()
            return c
        lax.fori_loop(0, self.n_pages, body, 0)
        for cp in self._tail(b, slot):
            cp.start()

    def wait(self, b, slot):
        def body(p, c):
            for cp in self._page(b, p, slot):
                cp.wait()
            return c
        lax.fori_loop(0, self.n_pages, body, 0)
        for cp in self._tail(b, slot):
            cp.wait()

    def fetch(self):
        b = pl.program_id(0)
        slot = b & 1

        @pl.when(b == 0)
        def _():
            self.start(b, slot)

        @pl.when(b + 1 < pl.num_programs(0))
        def _():
            self.start(b + 1, 1 - slot)

        self.wait(b, slot)
        return slot

    def half(self, slot, i):
        return self.bufs[i].at[pl.ds(slot, 1)]


def _gelu_tanh(x):
    return 0.5 * x * (1.0 + jnp.tanh(math.sqrt(2.0 / math.pi) * (x + 0.044715 * (x * x * x))))


def _compress_paged_kernel(pt_ref, cache_ref, tail_ref, mix_ref, pe_ref, w1_ref, w2_ref, kc_ref, vc_ref,
                           kbuf_ref, vbuf_ref, sem_ref):
    rows = _PagedRows(pt_ref, cache_ref, tail_ref, (kbuf_ref, vbuf_ref), sem_ref, 0)
    slot = rows.fetch()
    _compress_kernel(rows.half(slot, 0), rows.half(slot, 1), mix_ref, pe_ref, w1_ref, w2_ref, kc_ref, vc_ref)


def _compress_paged(page_table, cache2d, tail, mix2, pe2, w1bd, w2bd):
    db, n_pages = page_table.shape
    l = (n_pages + 1) * PAGE
    nch = _round_up(l // CMP_STRIDE, LANES)
    const3 = lambda i, pt: (0, 0, 0)
    any_spec = pl.BlockSpec(memory_space=pl.ANY)
    out = jax.ShapeDtypeStruct((db, nch, LANES), F32)
    return pl.pallas_call(
        _compress_paged_kernel,
        grid_spec=pltpu.PrefetchScalarGridSpec(
            num_scalar_prefetch=1, grid=(db,),
            in_specs=[any_spec, any_spec,
                      pl.BlockSpec((2, CMP_BLOCK, LANES), const3), pl.BlockSpec((2, CMP_BLOCK, LANES), const3),
                      pl.BlockSpec((2, LANES, LANES), const3), pl.BlockSpec((2, LANES, LANES), const3)],
            out_specs=[pl.BlockSpec((1, nch, LANES), lambda i, pt: (i, 0, 0))] * 2,
            scratch_shapes=[pltpu.VMEM((2, l, KV_LANES), cache2d.dtype)] * 2 + [pltpu.SemaphoreType.DMA((2,))]),
        out_shape=(out, out),
        compiler_params=_cparams(("arbitrary",)),
        name="compress_paged",
    )(page_table, cache2d, tail, mix2, pe2, w1bd, w2bd)


def _compress_kernel(rk_ref, rv_ref, mix_ref, pe_ref, w1_ref, w2_ref, kc_ref, vc_ref):
    nch = rk_ref.shape[1] // CMP_STRIDE
    nch_pad = kc_ref.shape[1]
    for idx, (r_ref, o_ref) in enumerate(((rk_ref, kc_ref), (rv_ref, vc_ref))):
        mix = mix_ref[idx]
        cs0 = jnp.zeros((nch, LANES), F32)
        cs1 = jnp.zeros((nch, LANES), F32)
        for p in range(CMP_STRIDE):
            xp = r_ref[0, pl.ds(p, nch, stride=CMP_STRIDE), :].astype(F32)
            cs0 = cs0 + xp * mix[p:p + 1]
            cs1 = cs1 + xp * mix[CMP_STRIDE + p:CMP_STRIDE + p + 1]
        c0 = jnp.sum(mix * pe_ref[idx], axis=0, keepdims=True)
        pre = c0 + cs0 + pltpu.roll(cs1, nch - 1, 0)
        h = _gelu_tanh(jnp.dot(pre, w1_ref[idx], precision=HI, preferred_element_type=F32))
        o_ref[0, pl.ds(0, nch), :] = jnp.dot(h, w2_ref[idx], precision=HI, preferred_element_type=F32)
        if nch_pad > nch:
            o_ref[0, pl.ds(nch, nch_pad - nch), :] = jnp.zeros((nch_pad - nch, LANES), F32)


def _compress(rows3d, mix2, pe2, w1bd, w2bd):
    b, l, _ = rows3d.shape
    nch = _round_up(l // CMP_STRIDE, LANES)
    const3 = lambda i: (0, 0, 0)
    out = jax.ShapeDtypeStruct((b, nch, LANES), F32)
    return pl.pallas_call(
        _compress_kernel,
        grid=(b,),
        in_specs=[pl.BlockSpec((1, l, LANES), lambda i: (i, 0, 0)),
                  pl.BlockSpec((1, l, LANES), lambda i: (i, 0, 1)),
                  pl.BlockSpec((2, CMP_BLOCK, LANES), const3), pl.BlockSpec((2, CMP_BLOCK, LANES), const3),
                  pl.BlockSpec((2, LANES, LANES), const3), pl.BlockSpec((2, LANES, LANES), const3)],
        out_specs=[pl.BlockSpec((1, nch, LANES), lambda i: (i, 0, 0))] * 2,
        out_shape=(out, out),
        compiler_params=_cparams(("parallel",)),
        name="compress",
    )(rows3d, rows3d, mix2, pe2, w1bd, w2bd)


def _masked_softmax0(s, mask):
    s = jnp.where(mask, s, NEG_INF)
    m = jnp.max(s, axis=0, keepdims=True)
    e = jnp.where(mask, jnp.exp(s - m), 0.0)
    return e / jnp.maximum(jnp.sum(e, axis=0, keepdims=True), 1e-30)


def _flash_t(q_aug, k_ref, v_ref, lo, hi, k_aug_fn, mask_fn, kt_rows, zero_masked):
    r = q_aug.shape[1]

    def body(kt, carry):
        m, l, acc = carry
        off = pl.multiple_of(kt * kt_rows, kt_rows)
        k = k_aug_fn(kt, k_ref[0, pl.ds(off, kt_rows), :].astype(BF16))
        vt = v_ref[0, pl.ds(off, kt_rows), :].astype(F32).T.astype(BF16)
        s = jnp.dot(k, q_aug, preferred_element_type=F32)
        mask = mask_fn(kt)
        s = jnp.where(mask, s, NEG_INF)
        m_new = jnp.maximum(m, jnp.max(s, axis=0, keepdims=True))
        p = jnp.exp(s - m_new)
        if zero_masked:
            p = jnp.where(mask, p, 0.0)
        alpha = jnp.exp(m - m_new)
        l = alpha * l + jnp.sum(p, axis=0, keepdims=True)
        acc = alpha * acc + jnp.dot(vt, p.astype(BF16), preferred_element_type=F32)
        return m_new, l, acc

    init = (jnp.full((1, r), NEG_INF, F32), jnp.zeros((1, r), F32), jnp.zeros((LANES, r), F32))
    _, l, acc = lax.fori_loop(lo, hi, body, init)
    return acc / jnp.maximum(l, 1e-30)


def _nsa_kernel(*refs, **static):
    _nsa_body(pl.program_id(1), *refs, **static)


def _nsa_paged_kernel(pt_ref, qt_ref, gt_ref, kc_ref, vc_ref, cover_ref, cache_ref, tail_ref, kw_ref, vw_ref,
                      o_ref, kbuf_ref, vbuf_ref, sem_ref, **static):
    rows = _PagedRows(pt_ref, cache_ref, tail_ref, (kbuf_ref, vbuf_ref), sem_ref, 2)
    slot = rows.fetch()
    _nsa_body(0, qt_ref, gt_ref, kc_ref, vc_ref, cover_ref, rows.half(slot, 0), rows.half(slot, 1), kw_ref, vw_ref,
              o_ref, **static)


def _nsa_body(qi, qt_ref, gt_ref, kc_ref, vc_ref, cover_ref, ks_ref, vs_ref, kw_ref, vw_ref, o_ref,
              *, tq, q_start, kw_start, n_top, kt_rows):
    r = NSA_HEADS * tq
    t0 = q_start + qi * tq
    qt = qt_ref[0, 0]
    t_col = t0 + (lax.broadcasted_iota(jnp.int32, (1, r), 1) & (tq - 1))

    kc = kc_ref[0].astype(BF16)
    nch = kc.shape[0]
    s_c = jnp.dot(kc, qt, preferred_element_type=F32)
    cmp_end = lax.broadcasted_iota(jnp.int32, (nch, 1), 0) * CMP_STRIDE + (CMP_BLOCK - 1)
    p_c = _masked_softmax0(s_c, cmp_end <= t_col)
    o_c = jnp.dot(vc_ref[0].T.astype(BF16), p_c.astype(BF16), preferred_element_type=F32)

    ns_pad = cover_ref.shape[0]
    cur = (t0 + lax.broadcasted_iota(jnp.int32, (1, tq), 1)) >> 6
    jb = lax.broadcasted_iota(jnp.int32, (ns_pad, 1), 0)
    jbf = jb.astype(F32)
    elig = jb <= cur
    forced = (jb == 0) | (jb == cur) | (jb == cur - 1)
    notsel = []
    for g in range(NSA_KV_HEADS):
        base = g * GROUP * tq
        pg = p_c[:, base:base + tq]
        for n in range(1, GROUP):
            pg = pg + p_c[:, base + n * tq:base + (n + 1) * tq]
        imp = jnp.dot(cover_ref[...], pg, precision=HI, preferred_element_type=F32)
        score = jnp.where(elig, jnp.where(forced, SEL_FORCE, imp), -SEL_FORCE)
        sel = jnp.zeros((ns_pad, tq), F32)
        for _ in range(n_top):
            m = jnp.max(score, axis=0, keepdims=True)
            first = jnp.min(jnp.where(score == m, jbf, 1e9), axis=0, keepdims=True)
            hit = jbf == first
            sel = jnp.where(hit, 1.0, sel)
            score = jnp.where(hit, -3e38, score)
        notsel.append(jnp.where(elig, 1.0 - sel, 1.0).astype(BF16))
    notsel_cols = jnp.concatenate([notsel[0]] * GROUP + [notsel[1]] * GROUP, axis=1)

    q_aug = jnp.concatenate([qt, notsel_cols], axis=0)

    def sel_keys(kt, k):
        blk = kt * (kt_rows // SEL_BLOCK) + (lax.broadcasted_iota(jnp.int32, (kt_rows, ns_pad), 0) >> 6)
        tag = jnp.where(lax.broadcasted_iota(jnp.int32, (kt_rows, ns_pad), 1) == blk, -SEL_MASK_BIG, 0.0)
        return jnp.concatenate([k, tag.astype(BF16)], axis=1)

    def sel_mask(kt):
        kpos = kt * kt_rows + lax.broadcasted_iota(jnp.int32, (kt_rows, 1), 0)
        return kpos <= t_col

    hi_s = (t0 + tq - 1) // kt_rows + 1
    o_s = _flash_t(q_aug, ks_ref, vs_ref, 0, hi_s, sel_keys, sel_mask, kt_rows, zero_masked=False)

    def win_mask(kt):
        kpos = kw_start + kt * kt_rows + lax.broadcasted_iota(jnp.int32, (kt_rows, 1), 0)
        dist = t_col - kpos
        return (dist >= 0) & (dist <= WINDOW)

    n_wt = kw_ref.shape[1] // kt_rows
    lo_w = jnp.maximum(t0 - WINDOW - kw_start, 0) // kt_rows
    hi_w = jnp.minimum((t0 + tq - 1 - kw_start) // kt_rows + 1, n_wt)
    o_w = _flash_t(qt, kw_ref, vw_ref, lo_w, hi_w, lambda kt, k: k, win_mask, kt_rows, zero_masked=True)

    gt = gt_ref[0]
    for h in range(NSA_HEADS):
        g = h // GROUP
        out = None
        for br, o in enumerate((o_c, o_s, o_w)):
            gate = gt[br * NSA_HEADS + h:br * NSA_HEADS + h + 1, :]
            term = gate * o[g * HEAD_DIM:(g + 1) * HEAD_DIM, h * tq:(h + 1) * tq]
            out = term if out is None else out + term
        o_ref[0, pl.ds(h * HEAD_DIM, HEAD_DIM), :] = out.astype(o_ref.dtype)


def _cover_matrix_t(nch, ns_pad):
    ci = np.arange(nch)[None, :] * CMP_STRIDE
    sj = np.arange(ns_pad)[:, None] * SEL_BLOCK
    return jnp.asarray(((ci < sj + SEL_BLOCK) & (ci + CMP_BLOCK > sj)).astype(np.float32))


def _block_diag_qt(q, b, sq, tq):
    nqt = sq // tq
    qr = q.reshape(b, nqt, tq, NSA_KV_HEADS, GROUP, HEAD_DIM).transpose(0, 1, 3, 5, 4, 2)
    z = jnp.zeros_like(qr[:, :, 0])
    top = jnp.stack([qr[:, :, 0], z], axis=3)
    bot = jnp.stack([z, qr[:, :, 1]], axis=3)
    return jnp.stack([top, bot], axis=2).reshape(b, nqt, LANES, NSA_HEADS * tq)


def _nsa_specs(nch, ns_pad, lw, tq, kw_col, idx):
    r = NSA_HEADS * tq
    return dict(
        qt=pl.BlockSpec((1, 1, LANES, r), idx(lambda i, j: (i, j, 0, 0))),
        gt=pl.BlockSpec((1, GN_PAD, tq), idx(lambda i, j: (i, 0, j))),
        kc=pl.BlockSpec((1, nch, LANES), idx(lambda i, j: (i, 0, 0))),
        cover=pl.BlockSpec((ns_pad, nch), idx(lambda i, j: (0, 0))),
        kw=pl.BlockSpec((1, lw, LANES), idx(lambda i, j: (i, 0, kw_col))),
        vw=pl.BlockSpec((1, lw, LANES), idx(lambda i, j: (i, 0, kw_col + 1))),
        out=pl.BlockSpec((1, NSA_WIDTH, tq), idx(lambda i, j: (i, 0, j))))


def _nsa(qt, gt, kc, vc, ksv, ks_col, kwv, kw_col, *, tq, q_start, kw_start, ns_true):
    b, nqt = qt.shape[:2]
    nch = kc.shape[1]
    lk = ksv.shape[1]
    ns_pad = _round_up(lk // SEL_BLOCK, LANES)
    sp = _nsa_specs(nch, ns_pad, kwv.shape[1], tq, kw_col, lambda f: f)
    kt_rows = 2 * KEY_TILE if lk % (2 * KEY_TILE) == 0 and kwv.shape[1] % (2 * KEY_TILE) == 0 else KEY_TILE
    kern = functools.partial(_nsa_kernel, tq=tq, q_start=q_start, kw_start=kw_start, n_top=min(SEL_TOP, ns_true),
                             kt_rows=kt_rows)
    return pl.pallas_call(
        kern,
        grid=(b, nqt),
        in_specs=[sp["qt"], sp["gt"], sp["kc"], sp["kc"], sp["cover"],
                  pl.BlockSpec((1, lk, LANES), lambda i, j: (i, 0, ks_col)),
                  pl.BlockSpec((1, lk, LANES), lambda i, j: (i, 0, ks_col + 1)),
                  sp["kw"], sp["vw"]],
        out_specs=sp["out"],
        out_shape=jax.ShapeDtypeStruct((b, NSA_WIDTH, nqt * tq), BF16),
        compiler_params=_cparams(("parallel", "arbitrary")),
        name="nsa",
    )(qt, gt, kc, vc, _cover_matrix_t(nch, ns_pad), ksv, ksv, kwv, kwv)


def _nsa_paged(qt, gt, kc, vc, page_table, cache2d, tail, kwv, *, tq, q_start, kw_start, ns_true):
    db, nqt = qt.shape[:2]
    assert nqt == 1
    nch = kc.shape[1]
    lk = (page_table.shape[1] + 1) * PAGE
    ns_pad = _round_up(lk // SEL_BLOCK, LANES)
    sp = _nsa_specs(nch, ns_pad, kwv.shape[1], tq, 0, lambda f: (lambda i, pt: f(i, 0)))
    kern = functools.partial(_nsa_paged_kernel, tq=tq, q_start=q_start, kw_start=kw_start, n_top=min(SEL_TOP, ns_true),
                             kt_rows=KEY_TILE)
    any_spec = pl.BlockSpec(memory_space=pl.ANY)
    return pl.pallas_call(
        kern,
        grid_spec=pltpu.PrefetchScalarGridSpec(
            num_scalar_prefetch=1, grid=(db,),
            in_specs=[sp["qt"], sp["gt"], sp["kc"], sp["kc"], sp["cover"], any_spec, any_spec, sp["kw"], sp["vw"]],
            out_specs=sp["out"],
            scratch_shapes=[pltpu.VMEM((2, lk, KV_LANES), cache2d.dtype)] * 2 + [pltpu.SemaphoreType.DMA((2,))]),
        out_shape=jax.ShapeDtypeStruct((db, NSA_WIDTH, tq), BF16),
        compiler_params=_cparams(("arbitrary",)),
        name="nsa_paged",
    )(page_table, qt, gt, kc, vc, _cover_matrix_t(nch, ns_pad), cache2d, tail, kwv, kwv)


def _rwkv_pre_kernel(c_ref, p_ref, mu_ref, w0_ref, a0_ref, kkw_ref, ka_ref, wl_ref, g2_ref,
                     r_ref, k_ref, v_ref, kk_ref, a_ref, d_ref, g_ref):
    cols = c_ref[...]
    xs = cols + mu_ref[...] * (p_ref[...] - cols)
    o1, o2, o3 = RW_WIDTH, 2 * RW_WIDTH, 3 * RW_WIDTH
    o5 = o3 + RW_DECAY_LORA + RW_AAA_LORA
    k = xs[:, o1:o2]
    wa = xs[:, o3:o5]
    lane = lax.broadcasted_iota(jnp.int32, wa.shape, 1)
    wa = jnp.where(lane < RW_DECAY_LORA, jnp.tanh(wa), wa)
    lo = jnp.dot(wa.astype(BF16), wl_ref[...], preferred_element_type=F32)
    z = -(w0_ref[...] + lo[:, :RW_WIDTH])
    softplus = jnp.maximum(z, 0.0) + jnp.log(1.0 + jnp.exp(-jnp.abs(z)))
    w = -softplus - 0.5
    a = _sigmoid(a0_ref[...] + lo[:, RW_WIDTH:])
    r_ref[...] = xs[:, :o1]
    k_ref[...] = k * (1.0 + (a - 1.0) * ka_ref[...])
    v_ref[...] = xs[:, o2:o3]
    kk_ref[...] = k * kkw_ref[...]
    a_ref[...] = a
    d_ref[...] = jnp.exp(-jnp.exp(w))
    g_ref[...] = jnp.dot(_sigmoid(xs[:, o5:]).astype(BF16), g2_ref[...], preferred_element_type=F32)


def _rwkv_pre(cols, prev, row0, nrows, mu, w0, a0, kkw, ka, wlora, g2):
    tm = TOK_TILE
    assert row0 % tm == 0 and nrows % tm == 0 and row0 + nrows <= cols.shape[0]
    tile0 = row0 // tm
    row = lambda i: (i, 0)
    src = lambda i: (i + tile0, 0)
    const = lambda i: (0, 0)
    vec = pl.BlockSpec((1, RW_WIDTH), const)
    out = jax.ShapeDtypeStruct((nrows, RW_WIDTH), F32)
    return pl.pallas_call(
        _rwkv_pre_kernel,
        grid=(nrows // tm,),
        in_specs=[pl.BlockSpec((tm, RW_COLS), src), pl.BlockSpec((tm, RW_COLS), src),
                  pl.BlockSpec((1, RW_COLS), const), vec, vec, vec, vec,
                  pl.BlockSpec((RW_DECAY_LORA + RW_AAA_LORA, 2 * RW_WIDTH), const),
                  pl.BlockSpec((RW_GATE_LORA, RW_WIDTH), const)],
        out_specs=[pl.BlockSpec((tm, RW_WIDTH), row)] * 7,
        out_shape=(out,) * 7,
        compiler_params=_cparams(("parallel",)),
        name="rwkv_pre",
    )(cols, prev, mu, w0, a0, kkw, ka, wlora, g2)


def _rwkv_scan_kernel(r_ref, k_ref, v_ref, kk_ref, a_ref, d_ref, g_ref, s0_ref, lnw_ref, lnb_ref, rk_ref,
                      o_ref, sf_ref, st_ref, kkn_ref, b_ref):
    n = RW_HEAD_DIM
    tb = r_ref.shape[0]
    ti = pl.program_id(1)

    @pl.when(ti == 0)
    def _():
        st_ref[...] = s0_ref[...]

    def step(t, c):
        kk = kk_ref[t]
        nrm = jnp.sqrt(jnp.sum(kk * kk, axis=0, keepdims=True))
        kkn = kk / jnp.maximum(nrm, 1e-12)
        kkn_ref[...] = kkn
        b_ref[...] = kkn * a_ref[t]
        vv = v_ref[t]
        sa = jnp.zeros((n, LANES), F32)
        for i in range(n):
            sa = sa - st_ref[i] * kkn_ref[pl.ds(i, 1), :]
        y = jnp.zeros((n, LANES), F32)
        for i in range(n):
            s_new = (st_ref[i] * d_ref[t, pl.ds(i, 1), :] + sa * b_ref[pl.ds(i, 1), :]
                     + vv * k_ref[t, pl.ds(i, 1), :])
            st_ref[i] = s_new
            y = y + s_new * r_ref[t, pl.ds(i, 1), :]
        mu = jnp.mean(y, axis=0, keepdims=True)
        yc = y - mu
        var = jnp.mean(yc * yc, axis=0, keepdims=True)
        yn = yc * lax.rsqrt(var + GN_EPS) * lnw_ref[...] + lnb_ref[...]
        bonus = jnp.sum(r_ref[t] * k_ref[t] * rk_ref[...], axis=0, keepdims=True) * vv
        o_ref[t] = (yn + bonus) * g_ref[t]
        return c

    lax.fori_loop(0, tb, step, 0)

    @pl.when(ti == pl.num_programs(1) - 1)
    def _():
        sf_ref[...] = st_ref[...]


def _rwkv_scan(seqs, s0, lnw, lnb, rk, tb):
    s, n, nl = seqs[0].shape
    blk = pl.BlockSpec((tb, n, LANES), lambda j, i: (i, 0, j))
    st = pl.BlockSpec((n, n, LANES), lambda j, i: (0, 0, j))
    vec = pl.BlockSpec((n, LANES), lambda j, i: (0, j))
    return pl.pallas_call(
        _rwkv_scan_kernel,
        grid=(nl // LANES, s // tb),
        in_specs=[blk] * 7 + [st, vec, vec, vec],
        out_specs=[blk, st],
        out_shape=(jax.ShapeDtypeStruct((s, n, nl), F32), jax.ShapeDtypeStruct((n, n, nl), F32)),
        scratch_shapes=[pltpu.VMEM((n, n, LANES), F32), pltpu.VMEM((n, LANES), F32), pltpu.VMEM((n, LANES), F32)],
        compiler_params=_cparams(("parallel", "arbitrary")),
        name="rwkv_scan",
    )(*seqs, s0, lnw, lnb, rk)


def _to_scan_layout(x, b, s):
    return x.reshape(b, s, RW_HEADS, RW_HEAD_DIM).transpose(1, 3, 0, 2).reshape(s, RW_HEAD_DIM, b * RW_HEADS)


def _from_scan_layout(y, b, s):
    return y.reshape(s, RW_HEAD_DIM, b, RW_HEADS).transpose(2, 0, 3, 1).reshape(b * s, RW_WIDTH)


def _rwkv_group(pre, b, s, wkv0, lnw, lnb, rk):
    nl = b * RW_HEADS
    nlp = _round_up(nl, LANES)

    def padl(x):
        return x if nlp == nl else jnp.pad(x, [(0, 0)] * (x.ndim - 1) + [(0, nlp - nl)])

    seqs = [padl(_to_scan_layout(x, b, s)) for x in pre]
    s0 = padl(wkv0.astype(F32).transpose(3, 2, 0, 1).reshape(RW_HEAD_DIM, RW_HEAD_DIM, nl))
    tile = lambda p: padl(jnp.tile(p.reshape(RW_HEADS, RW_HEAD_DIM).T, (1, b)))
    tb = math.gcd(s, 32)
    o, sf = _rwkv_scan(seqs, s0, tile(lnw), tile(lnb), tile(rk), tb)
    o = _from_scan_layout(o[:, :, :nl], b, s)
    sf = sf[:, :, :nl].reshape(RW_HEAD_DIM, RW_HEAD_DIM, b, RW_HEADS).transpose(2, 3, 1, 0)
    return o, sf


def _merge_kernel(x_ref, on_ref, or_ref, mg_ref, wpa_ref, wpb_ref, wo_ref, n2_ref, wr_ref, br_ref,
                  h_ref, xn_ref, te_ref, tg_ref):
    pa = jnp.dot(on_ref[...], wpa_ref[...], preferred_element_type=F32)
    pb = jnp.dot(or_ref[...].astype(BF16), wpb_ref[...], preferred_element_type=F32)
    merged = mg_ref[:, :D_MODEL] * pa + mg_ref[:, D_MODEL:] * pb
    h = x_ref[...] + jnp.dot(merged.astype(BF16), wo_ref[...], preferred_element_type=F32)
    h_ref[...] = h
    ms = jnp.mean(h * h, axis=-1, keepdims=True)
    xn = h * lax.rsqrt(ms + RMS_EPS) * n2_ref[...]
    xn_ref[...] = xn
    logits = jnp.dot(xn, wr_ref[...], precision=HI, preferred_element_type=F32) + br_ref[...]
    lane = lax.broadcasted_iota(jnp.int32, logits.shape, 1)
    lanef = lane.astype(F32)
    te = jnp.zeros(logits.shape, F32)
    tv = jnp.full(logits.shape, NEG_INF, F32)
    for kx in range(TOP_K):
        m = jnp.max(logits, axis=-1, keepdims=True)
        first = jnp.min(jnp.where(logits == m, lanef, 1e9), axis=-1, keepdims=True)
        te = jnp.where(lane == kx, first, te)
        tv = jnp.where(lane == kx, m, tv)
        logits = jnp.where(lanef == first, -3e38, logits)
    e = jnp.where(lane < TOP_K, jnp.exp(tv - jnp.max(tv, axis=-1, keepdims=True)), 0.0)
    te_ref[...] = te.astype(jnp.int32)
    tg_ref[...] = e / jnp.sum(e, axis=-1, keepdims=True)


def _merge(x, o_nsa, o_rw, mg, wpa, wpb, wo, norm2, wr_pad, br_pad):
    t = x.shape[0]
    tm = TOK_TILE
    row = lambda i: (i, 0)
    const = lambda i: (0, 0)
    outs = (jax.ShapeDtypeStruct((t, D_MODEL), F32), jax.ShapeDtypeStruct((t, D_MODEL), F32),
            jax.ShapeDtypeStruct((t, LANES), jnp.int32), jax.ShapeDtypeStruct((t, LANES), F32))
    return pl.pallas_call(
        _merge_kernel,
        grid=(t // tm,),
        in_specs=[pl.BlockSpec((tm, D_MODEL), row), pl.BlockSpec((tm, NSA_WIDTH), row),
                  pl.BlockSpec((tm, RW_WIDTH), row), pl.BlockSpec((tm, 2 * D_MODEL), row),
                  pl.BlockSpec((NSA_WIDTH, D_MODEL), const), pl.BlockSpec((RW_WIDTH, D_MODEL), const),
                  pl.BlockSpec((D_MODEL, D_MODEL), const), pl.BlockSpec((1, D_MODEL), const),
                  pl.BlockSpec((D_MODEL, LANES), const), pl.BlockSpec((1, LANES), const)],
        out_specs=[pl.BlockSpec((tm, o.shape[1]), row) for o in outs],
        out_shape=outs,
        compiler_params=_cparams(("parallel",)),
        name="merge",
    )(x, o_nsa, o_rw, mg, wpa, wpb, wo, norm2.reshape(1, D_MODEL), wr_pad, br_pad)


class _GatheredRows:
    def __init__(self, src_ref, buf_ref, sem_ref):
        self.src, self.buf, self.sem = src_ref, buf_ref, sem_ref
        self.n = buf_ref.shape[1]

    def start(self, idx_ref, off, slot):
        def body(j, c):
            pltpu.make_async_copy(self.src.at[pl.ds(idx_ref[off + j], 1)], self.buf.at[slot, pl.ds(j, 1)],
                                  self.sem.at[slot]).start()
            return c
        lax.fori_loop(0, self.n, body, 0, unroll=8)

    def wait(self, slot):
        pltpu.make_async_copy(self.src.at[pl.ds(0, self.n)], self.buf.at[slot], self.sem.at[slot]).wait()


def _expert_kernel(be_ref, nu_ref, idx_ref, idxn_ref, xn_ref, wu_ref, bu_ref, wd_ref, bd_ref, o_ref, xbuf_ref, sem_ref):
    i = pl.program_id(0)
    n_used = nu_ref[0]
    slot = i & 1
    bm = xbuf_ref.shape[1]
    rows = _GatheredRows(xn_ref, xbuf_ref, sem_ref)

    @pl.when(i == 0)
    def _():
        rows.start(idx_ref, 0, slot)

    @pl.when(i + 1 < n_used)
    def _():
        rows.start(idxn_ref, (1 - slot) * bm, 1 - slot)

    @pl.when(i < n_used)
    def _():
        rows.wait(slot)
        x = xbuf_ref[slot].astype(BF16)
        hu = jnp.dot(x, wu_ref[0], preferred_element_type=F32) + bu_ref[0]
        gt = jnp.minimum(hu[:, :D_FF], SWIGLU_LIMIT)
        up = jnp.clip(hu[:, D_FF:], -SWIGLU_LIMIT, SWIGLU_LIMIT)
        hh = (up + 1.0) * gt * _sigmoid(SWIGLU_ALPHA * gt)
        o_ref[...] = jnp.dot(hh.astype(BF16), wd_ref[0], preferred_element_type=F32) + bd_ref[0]

    @pl.when(i >= n_used)
    def _():
        o_ref[...] = jnp.zeros(o_ref.shape, o_ref.dtype)


def _experts(blk_exp, n_used, slot_tok, xn, w_up, b_up, w_down, b_down):
    p = slot_tok.shape[0]
    bm = MOE_BM
    nb = p // bm
    assert nb % 2 == 0
    wmap = lambda i, be, nu: (be[i], 0, 0)
    return pl.pallas_call(
        _expert_kernel,
        grid_spec=pltpu.PrefetchScalarGridSpec(
            num_scalar_prefetch=2, grid=(nb,),
            in_specs=[pl.BlockSpec((2 * bm,), lambda i, be, nu: (i // 2,), memory_space=pltpu.SMEM),
                      pl.BlockSpec((2 * bm,), lambda i, be, nu: (jnp.minimum(i + 1, nb - 1) // 2,),
                                   memory_space=pltpu.SMEM),
                      pl.BlockSpec(memory_space=pl.ANY),
                      pl.BlockSpec((1, D_MODEL, 2 * D_FF), wmap), pl.BlockSpec((1, 1, 2 * D_FF), wmap),
                      pl.BlockSpec((1, D_FF, D_MODEL), wmap), pl.BlockSpec((1, 1, D_MODEL), wmap)],
            out_specs=pl.BlockSpec((bm, D_MODEL), lambda i, be, nu: (i, 0)),
            scratch_shapes=[pltpu.VMEM((2, bm, D_MODEL), xn.dtype), pltpu.SemaphoreType.DMA((2,))]),
        out_shape=jax.ShapeDtypeStruct((p, D_MODEL), F32),
        compiler_params=_cparams(("arbitrary",)),
        name="experts",
    )(blk_exp, n_used, slot_tok, slot_tok, xn, w_up, b_up.reshape(N_EXPERTS, 1, 2 * D_FF), w_down,
      b_down.reshape(N_EXPERTS, 1, D_MODEL))


def _final_kernel(idx_ref, idxn_ref, h_ref, yb_ref, tg_ref, nf_ref, o_ref, ybuf_ref, sem_ref):
    i = pl.program_id(0)
    slot = i & 1
    tm = h_ref.shape[0]
    rows = _GatheredRows(yb_ref, ybuf_ref, sem_ref)

    @pl.when(i == 0)
    def _():
        rows.start(idx_ref, 0, slot)

    @pl.when(i + 1 < pl.num_programs(0))
    def _():
        rows.start(idxn_ref, 0, 1 - slot)

    rows.wait(slot)
    tg = tg_ref[...]
    f = tg[:, 0:1] * ybuf_ref[slot, pl.ds(0, tm), :]
    for kx in range(1, TOP_K):
        f = f + tg[:, kx:kx + 1] * ybuf_ref[slot, pl.ds(kx * tm, tm), :]
    y = h_ref[...] + f
    ms = jnp.mean(y * y, axis=-1, keepdims=True)
    o_ref[...] = y * lax.rsqrt(ms + RMS_EPS) * nf_ref[...]


def _final(h, yb, slot_of, tg, norm_f):
    t = h.shape[0]
    tm = TOK_TILE
    nt = t // tm
    idx = slot_of.reshape(nt, tm, TOP_K).transpose(0, 2, 1).reshape(nt * TOP_K * tm)
    row = lambda i: (i, 0)
    return pl.pallas_call(
        _final_kernel,
        grid=(nt,),
        in_specs=[pl.BlockSpec((TOP_K * tm,), lambda i: (i,), memory_space=pltpu.SMEM),
                  pl.BlockSpec((TOP_K * tm,), lambda i: (jnp.minimum(i + 1, nt - 1),), memory_space=pltpu.SMEM),
                  pl.BlockSpec((tm, D_MODEL), row), pl.BlockSpec(memory_space=pl.ANY),
                  pl.BlockSpec((tm, LANES), row), pl.BlockSpec((1, D_MODEL), lambda i: (0, 0))],
        out_specs=pl.BlockSpec((tm, D_MODEL), row),
        out_shape=jax.ShapeDtypeStruct((t, D_MODEL), F32),
        scratch_shapes=[pltpu.VMEM((2, TOP_K * tm, D_MODEL), yb.dtype), pltpu.SemaphoreType.DMA((2,))],
        compiler_params=_cparams(("arbitrary",)),
        name="final",
    )(idx, idx, h, yb, tg, norm_f.reshape(1, D_MODEL))


def _routing(top_e, bm):
    t = top_e.shape[0]
    a = t * TOP_K
    e_flat = top_e.reshape(a)
    order = jnp.argsort(e_flat).astype(jnp.int32)
    e_sorted = e_flat[order]
    counts = jnp.bincount(e_flat, length=N_EXPERTS).astype(jnp.int32)
    starts = jnp.cumsum(counts) - counts
    padded = (counts + bm - 1) // bm * bm
    pends = jnp.cumsum(padded)
    pstarts = pends - padded
    slot_sorted = (pstarts[e_sorted] + jnp.arange(a, dtype=jnp.int32) - starts[e_sorted]).astype(jnp.int32)
    n_blocks = _round_up(-(-a // bm) + N_EXPERTS, IDX_BLOCKS)
    p = n_blocks * bm
    blk_exp = jnp.minimum(jnp.sum(jnp.arange(n_blocks)[:, None] * bm >= pends[None, :], axis=1),
                          N_EXPERTS - 1).astype(jnp.int32)
    slot = jnp.arange(p, dtype=jnp.int32)
    e_slot = jnp.repeat(blk_exp, bm)
    j = slot - pstarts[e_slot]
    src = jnp.clip(starts[e_slot] + j, 0, a - 1)
    slot_tok = jnp.where((j < counts[e_slot]) & (slot < pends[-1]), order[src] // TOP_K, 0).astype(jnp.int32)
    slot_of = slot_sorted[jnp.argsort(order)].reshape(t, TOP_K)
    n_used = (pends[-1:] // bm).astype(jnp.int32)
    return slot_tok, slot_of, blk_exp, n_used


def kernel(x_prompt, x_sample, cache_kv, cache_win, state_wkv, state_shift, page_table, norm1, w_in, cmp_pe, cmp_mix, cmp_w1, cmp_w2, rw_mu, rw_w0, rw_w2, rw_a0, rw_a2, rw_g2, rw_kk, rw_ka, rw_rk, rw_ln_w, rw_ln_b, w_pa, w_pb, w_o, norm2, w_router, b_router, w_up, b_up, w_down, b_down, norm_f):
    assert w_in.shape[0] == 1, "single-layer trunk"
    b, s, _ = x_prompt.shape
    db, ds, _ = x_sample.shape
    n_pages = page_table.shape[1]
    assert cache_kv.shape[2] == PAGE
    past = n_pages * PAGE
    nbuf = cache_win.shape[2]
    tp, ts = b * s, db * ds
    t_real = tp + ts
    t_pad = _round_up(t_real, TOK_ALIGN)

    wi = w_in[0]
    w_pad = jnp.concatenate(
        [wi[:, :OFF_GN], jnp.pad(wi[:, OFF_GN:OFF_RW], ((0, 0), (0, GN_PAD - 3 * NSA_HEADS))), wi[:, OFF_RW:]],
        axis=1).astype(BF16)
    tile_g = lambda p: jnp.tile(p, (1, 1, NSA_KV_HEADS))
    mix2, pe2 = tile_g(cmp_mix[0].astype(F32)), tile_g(cmp_pe[0].astype(F32))
    bd = lambda w: jnp.stack([jnp.kron(jnp.eye(NSA_KV_HEADS, dtype=F32), w[i].astype(F32)) for i in range(2)])
    w1bd, w2bd = bd(cmp_w1[0]), bd(cmp_w2[0])
    zl = jnp.zeros((RW_DECAY_LORA, RW_WIDTH), F32)
    wlora = jnp.concatenate([jnp.concatenate([rw_w2[0], zl], axis=1),
                             jnp.concatenate([zl, rw_a2[0]], axis=1)], axis=0).astype(BF16)
    vec = lambda p: p[0].reshape(1, -1).astype(F32)
    wr_pad = jnp.pad(w_router[0].astype(F32), ((0, 0), (0, LANES - N_EXPERTS)))
    br_pad = jnp.concatenate([b_router[0].astype(F32), jnp.full((LANES - N_EXPERTS,), NEG_INF, F32)]).reshape(1, LANES)

    x_all = jnp.concatenate([x_prompt.reshape(tp, D_MODEL), x_sample.reshape(ts, D_MODEL)], axis=0)
    x_all = jnp.pad(x_all, ((0, t_pad - t_real), (0, 0)))
    q_all, kv_all, kvb_all, gn_all, rw_all, mg_all = _proj(x_all, norm1[0], w_pad)

    kv_p = kv_all[:tp].reshape(b, s, 6 * KV_LANES)
    kv_s = kv_all[tp:t_real].reshape(db, ds, 6 * KV_LANES)
    kvb_p = kvb_all[:tp].reshape(b, s, 6 * KV_LANES)

    tq = NSA_TQ
    assert s % tq == 0 and ds <= tq and ds <= PAGE
    kc_p, vc_p = _compress(kv_p, mix2, pe2, w1bd, w2bd)
    gt_p = gn_all[:tp].reshape(b, s, GN_PAD).transpose(0, 2, 1)
    o_nsa_p = _nsa(_block_diag_qt(q_all[:tp], b, s, tq), gt_p, kc_p, vc_p,
                   kvb_p, 2, kvb_p, 4, tq=tq, q_start=0, kw_start=0, ns_true=s // SEL_BLOCK)

    row_w = 4 * KV_LANES
    tail = jnp.pad(kv_s[:, :, :row_w], ((0, 0), (0, PAGE - ds), (0, 0)))
    cache2d = cache_kv[0].reshape(-1, PAGE, row_w)
    kc_s, vc_s = _compress_paged(page_table, cache2d, tail, mix2, pe2, w1bd, w2bd)
    kw_full = jnp.concatenate([cache_win[0].reshape(db, nbuf, 2 * KV_LANES), kv_s[:, :, row_w:]], axis=1)
    lw = _round_up(nbuf + ds, KEY_TILE)
    kw_pad = jnp.pad(kw_full, ((0, 0), (0, lw - (nbuf + ds)), (0, 0)))
    lp = _round_up(past + ds, SEL_BLOCK)
    pad_q = lambda x: jnp.pad(x.reshape(db, ds, -1), ((0, 0), (0, tq - ds), (0, 0)))
    qt_s = _block_diag_qt(pad_q(q_all[tp:t_real]).reshape(db * tq, NSA_WIDTH), db, tq, tq)
    gt_s = pad_q(gn_all[tp:t_real]).transpose(0, 2, 1)
    o_nsa_s = _nsa_paged(qt_s, gt_s, kc_s, vc_s, page_table, cache2d, tail, kw_pad,
                         tq=tq, q_start=past, kw_start=past - nbuf, ns_true=lp // SEL_BLOCK)
    o_nsa = jnp.concatenate([o_nsa_p.transpose(0, 2, 1).reshape(tp, NSA_WIDTH),
                             o_nsa_s[:, :, :ds].transpose(0, 2, 1).reshape(ts, NSA_WIDTH),
                             jnp.zeros((t_pad - t_real, NSA_WIDTH), BF16)], axis=0)

    rw_p = rw_all[:tp].reshape(b, s, RW_COLS)
    rw_s = rw_all[tp:t_real].reshape(db, ds, RW_COLS)
    prev_p = jnp.concatenate([jnp.zeros((b, 1, RW_COLS), F32), rw_p[:, :-1]], axis=1)
    prev_s = jnp.concatenate([state_shift[0][:, None].astype(F32), rw_s[:, :-1]], axis=1)
    prev_all = jnp.concatenate([prev_p.reshape(tp, RW_COLS), prev_s.reshape(ts, RW_COLS),
                                jnp.zeros((t_pad - t_real, RW_COLS), F32)], axis=0)
    assert tp % TOK_ALIGN == 0
    pre_args = (vec(rw_mu), vec(rw_w0), vec(rw_a0), vec(rw_kk), vec(rw_ka), wlora, rw_g2[0].astype(BF16))
    pre_p = _rwkv_pre(rw_all, prev_all, 0, tp, *pre_args)
    pre_s = _rwkv_pre(rw_all, prev_all, tp, _round_up(ts, TOK_TILE), *pre_args)
    o_rw_p, wkv_p = _rwkv_group(pre_p, b, s, jnp.zeros((b, RW_HEADS, RW_HEAD_DIM, RW_HEAD_DIM), F32),
                                rw_ln_w[0], rw_ln_b[0], rw_rk[0])
    o_rw_s, wkv_s = _rwkv_group([x[:ts] for x in pre_s], db, ds, state_wkv[0],
                                rw_ln_w[0], rw_ln_b[0], rw_rk[0])
    o_rw = jnp.concatenate([o_rw_p, o_rw_s, jnp.zeros((t_pad - t_real, RW_WIDTH), F32)], axis=0)

    h, xn2, te, tg = _merge(x_all, o_nsa, o_rw, mg_all, w_pa[0].astype(BF16), w_pb[0].astype(BF16),
                            w_o[0].astype(BF16), norm2[0], wr_pad, br_pad)

    slot_tok, slot_of, blk_exp, n_used = _routing(te[:, :TOP_K], MOE_BM)
    yb = _experts(blk_exp, n_used, slot_tok, xn2, w_up[0].astype(BF16), b_up[0].astype(F32), w_down[0].astype(BF16),
                  b_down[0].astype(F32))
    y_all = _final(h, yb, slot_of, tg, norm_f)

    y_prompt = y_all[:tp].reshape(b, s, D_MODEL)
    y_sample = y_all[tp:t_real].reshape(db, ds, D_MODEL)
    kv_shape = (4, NSA_KV_HEADS, HEAD_DIM)
    win_shape = (2, NSA_KV_HEADS, HEAD_DIM)
    kv_prompt = kv_p[:, :, :row_w].reshape((1, b, s) + kv_shape)
    kv_sample = kv_s[:, :, :row_w].reshape((1, db, ds) + kv_shape)
    keep_p = min(WINDOW, s)
    win_prompt = kv_p[:, s - keep_p:, row_w:].reshape((1, b, keep_p) + win_shape)
    win_sample = kw_full[:, nbuf + ds - nbuf:].reshape((1, db, nbuf) + win_shape)
    shift_prompt = rw_p[:, -1][None]
    shift_sample = rw_s[:, -1][None]
    return (y_prompt, y_sample, kv_prompt, kv_sample, win_prompt, win_sample,
            wkv_p[None], wkv_s[None], shift_prompt, shift_sample)
```

```python
import functools
import math

import jax
import jax.numpy as jnp
import numpy as np
from jax import lax
from jax.experimental import pallas as pl
from jax.experimental.pallas import tpu as pltpu

F32 = jnp.float32
BF16 = jnp.bfloat16
HI = lax.Precision.HIGHEST

D_MODEL = 1024
NSA_HEADS = 8
NSA_KV_HEADS = 2
HEAD_DIM = 64
GROUP = NSA_HEADS // NSA_KV_HEADS
NSA_WIDTH = NSA_HEADS * HEAD_DIM
KV_LANES = NSA_KV_HEADS * HEAD_DIM
CMP_BLOCK = 32
CMP_STRIDE = 16
SEL_BLOCK = 64
SEL_TOP = 16
WINDOW = 512
SEL_FORCE = 1e9
NEG_INF = -1e30
SEL_MASK_BIG = 2.0 ** 100
PAGE = 128

RW_HEADS = 8
RW_HEAD_DIM = 64
RW_WIDTH = RW_HEADS * RW_HEAD_DIM
RW_DECAY_LORA = 64
RW_AAA_LORA = 64
RW_GATE_LORA = 128
RW_COLS = 3 * RW_WIDTH + RW_DECAY_LORA + RW_AAA_LORA + RW_GATE_LORA
GN_EPS = 64e-5

N_EXPERTS = 32
TOP_K = 4
D_FF = 1024
SWIGLU_LIMIT = 7.0
SWIGLU_ALPHA = 1.702
RMS_EPS = 1e-5

OFF_KV = NSA_WIDTH
OFF_GN = OFF_KV + 6 * KV_LANES
OFF_RW = OFF_GN + 3 * NSA_HEADS
OFF_MG = OFF_RW + RW_COLS
N_PROJ = OFF_MG + 2 * D_MODEL

LANES = 128
GN_PAD = LANES
P_Q = 0
P_KV = P_Q + NSA_WIDTH
P_GN = P_KV + 6 * KV_LANES
P_RW = P_GN + GN_PAD
P_MG = P_RW + RW_COLS
P_END = P_MG + 2 * D_MODEL

VMEM_LIMIT = 52 * 1024 * 1024

TOK_TILE = 256
TOK_ALIGN = 512
MOE_BM = 512
IDX_BLOCKS = 2
KEY_TILE = 128
NSA_TQ = 128
NSA_TQ_FEW = 16
GATE_ROWS = 8


def _cparams(sem):
    return pltpu.CompilerParams(dimension_semantics=sem, vmem_limit_bytes=VMEM_LIMIT)


def _sigmoid(x):
    return 1.0 / (1.0 + jnp.exp(-x))


def _round_up(a, b):
    return -(-a // b) * b


def _proj_kernel(x_ref, g_ref, w_ref, q_ref, kv_ref, kvb_ref, gn_ref, rw_ref, mg_ref):
    x = x_ref[...]
    ms = jnp.mean(x * x, axis=-1, keepdims=True)
    xn = (x * lax.rsqrt(ms + RMS_EPS) * g_ref[...]).astype(BF16)

    def seg(a, b):
        return jnp.dot(xn, w_ref[:, a:b], preferred_element_type=F32)

    q_ref[...] = (seg(P_Q, P_KV) * (HEAD_DIM ** -0.5)).astype(BF16)
    kv = seg(P_KV, P_GN)
    kv_ref[...] = kv
    kvb_ref[...] = kv.astype(BF16)
    gn_ref[...] = _sigmoid(seg(P_GN, P_RW))
    rw_ref[...] = seg(P_RW, P_MG)
    mg_ref[...] = _sigmoid(seg(P_MG, P_END))


def _proj(x, norm1, w_pad):
    T = x.shape[0]
    tm = TOK_TILE
    row = lambda i: (i, 0)
    const = lambda i: (0, 0)
    outs = (
        jax.ShapeDtypeStruct((T, NSA_WIDTH), BF16),
        jax.ShapeDtypeStruct((T, 6 * KV_LANES), F32),
        jax.ShapeDtypeStruct((T, 6 * KV_LANES), BF16),
        jax.ShapeDtypeStruct((T, GN_PAD), F32),
        jax.ShapeDtypeStruct((T, RW_COLS), F32),
        jax.ShapeDtypeStruct((T, 2 * D_MODEL), F32),
    )
    return pl.pallas_call(
        _proj_kernel,
        grid=(T // tm,),
        in_specs=[pl.BlockSpec((tm, D_MODEL), row), pl.BlockSpec((1, D_MODEL), const),
                  pl.BlockSpec((D_MODEL, P_END), const)],
        out_specs=[pl.BlockSpec((tm, o.shape[1]), row) for o in outs],
        out_shape=outs,
        compiler_params=_cparams(("parallel",)),
        name="proj",
    )(x, norm1.reshape(1, D_MODEL), w_pad)


class _PagedRows:
    def __init__(self, pt_ref, cache_ref, tail_ref, buf_refs, sem_ref, col0):
        self.pt, self.cache, self.tail, self.bufs, self.sem, self.col0 = pt_ref, cache_ref, tail_ref, buf_refs, sem_ref, col0
        self.n_pages = pt_ref.shape[1]

    def _copies(self, src, row0, slot):
        return [pltpu.make_async_copy(src.at[:, pl.ds((self.col0 + i) * KV_LANES, KV_LANES)],
                                      buf.at[slot, pl.ds(row0, PAGE), :], self.sem.at[slot])
                for i, buf in enumerate(self.bufs)]

    def _page(self, b, p, slot):
        return self._copies(self.cache.at[self.pt[b, p]], p * PAGE, slot)

    def _tail(self, b, slot):
        return self._copies(self.tail.at[b], self.n_pages * PAGE, slot)

    def start(self, b, slot):
        def body(p, c):
            for cp in self._page(b, p, slot):
                cp.start()
            return c
        lax.fori_loop(0, self.n_pages, body, 0)
        for cp in self._tail(b, slot):
            cp.start()

    def wait(self, b, slot):
        def body(p, c):
            for cp in self._page(b, p, slot):
                cp.wait()
            return c
        lax.fori_loop(0, self.n_pages, body, 0)
        for cp in self._tail(b, slot):
            cp.wait()

    def fetch(self):
        b = pl.program_id(0)
        slot = b & 1

        @pl.when(b == 0)
        def _():
            self.start(b, slot)

        @pl.when(b + 1 < pl.num_programs(0))
        def _():
            self.start(b + 1, 1 - slot)

        self.wait(b, slot)
        return slot

    def half(self, slot, i):
        return self.bufs[i].at[pl.ds(slot, 1)]


def _gelu_tanh(x):
    return 0.5 * x * (1.0 + jnp.tanh(math.sqrt(2.0 / math.pi) * (x + 0.044715 * (x * x * x))))


def _compress_paged_kernel(pt_ref, cache_ref, tail_ref, mix_ref, pe_ref, w1_ref, w2_ref, kc_ref, vc_ref,
                           kbuf_ref, vbuf_ref, sem_ref):
    rows = _PagedRows(pt_ref, cache_ref, tail_ref, (kbuf_ref, vbuf_ref), sem_ref, 0)
    slot = rows.fetch()
    _compress_kernel(rows.half(slot, 0), rows.half(slot, 1), mix_ref, pe_ref, w1_ref, w2_ref, kc_ref, vc_ref)


def _compress_paged(page_table, cache2d, tail, mix2, pe2, w1bd, w2bd):
    db, n_pages = page_table.shape
    l = (n_pages + 1) * PAGE
    nch = _round_up(l // CMP_STRIDE, LANES)
    const3 = lambda i, pt: (0, 0, 0)
    any_spec = pl.BlockSpec(memory_space=pl.ANY)
    out = jax.ShapeDtypeStruct((db, nch, LANES), F32)
    return pl.pallas_call(
        _compress_paged_kernel,
        grid_spec=pltpu.PrefetchScalarGridSpec(
            num_scalar_prefetch=1, grid=(db,),
            in_specs=[any_spec, any_spec,
                      pl.BlockSpec((2, CMP_BLOCK, LANES), const3), pl.BlockSpec((2, CMP_BLOCK, LANES), const3),
                      pl.BlockSpec((2, LANES, LANES), const3), pl.BlockSpec((2, LANES, LANES), const3)],
            out_specs=[pl.BlockSpec((1, nch, LANES), lambda i, pt: (i, 0, 0))] * 2,
            scratch_shapes=[pltpu.VMEM((2, l, KV_LANES), cache2d.dtype)] * 2 + [pltpu.SemaphoreType.DMA((2,))]),
        out_shape=(out, out),
        compiler_params=_cparams(("arbitrary",)),
        name="compress_paged",
    )(page_table, cache2d, tail, mix2, pe2, w1bd, w2bd)


def _compress_kernel(rk_ref, rv_ref, mix_ref, pe_ref, w1_ref, w2_ref, kc_ref, vc_ref):
    nch = rk_ref.shape[1] // CMP_STRIDE
    nch_pad = kc_ref.shape[1]
    for idx, (r_ref, o_ref) in enumerate(((rk_ref, kc_ref), (rv_ref, vc_ref))):
        mix = mix_ref[idx]
        cs0 = jnp.zeros((nch, LANES), F32)
        cs1 = jnp.zeros((nch, LANES), F32)
        for p in range(CMP_STRIDE):
            xp = r_ref[0, pl.ds(p, nch, stride=CMP_STRIDE), :].astype(F32)
            cs0 = cs0 + xp * mix[p:p + 1]
            cs1 = cs1 + xp * mix[CMP_STRIDE + p:CMP_STRIDE + p + 1]
        c0 = jnp.sum(mix * pe_ref[idx], axis=0, keepdims=True)
        pre = c0 + cs0 + pltpu.roll(cs1, nch - 1, 0)
        h = _gelu_tanh(jnp.dot(pre, w1_ref[idx], precision=HI, preferred_element_type=F32))
        o_ref[0, pl.ds(0, nch), :] = jnp.dot(h, w2_ref[idx], precision=HI, preferred_element_type=F32)
        if nch_pad > nch:
            o_ref[0, pl.ds(nch, nch_pad - nch), :] = jnp.zeros((nch_pad - nch, LANES), F32)


def _compress(rows3d, mix2, pe2, w1bd, w2bd):
    b, l, _ = rows3d.shape
    nch = _round_up(l // CMP_STRIDE, LANES)
    const3 = lambda i: (0, 0, 0)
    out = jax.ShapeDtypeStruct((b, nch, LANES), F32)
    return pl.pallas_call(
        _compress_kernel,
        grid=(b,),
        in_specs=[pl.BlockSpec((1, l, LANES), lambda i: (i, 0, 0)),
                  pl.BlockSpec((1, l, LANES), lambda i: (i, 0, 1)),
                  pl.BlockSpec((2, CMP_BLOCK, LANES), const3), pl.BlockSpec((2, CMP_BLOCK, LANES), const3),
                  pl.BlockSpec((2, LANES, LANES), const3), pl.BlockSpec((2, LANES, LANES), const3)],
        out_specs=[pl.BlockSpec((1, nch, LANES), lambda i: (i, 0, 0))] * 2,
        out_shape=(out, out),
        compiler_params=_cparams(("parallel",)),
        name="compress",
    )(rows3d, rows3d, mix2, pe2, w1bd, w2bd)


def _masked_softmax0(s, mask):
    s = jnp.where(mask, s, NEG_INF)
    m = jnp.max(s, axis=0, keepdims=True)
    e = jnp.where(mask, jnp.exp(s - m), 0.0)
    return e / jnp.maximum(jnp.sum(e, axis=0, keepdims=True), 1e-30)


def _flash_t(q_aug, k_ref, v_ref, lo, hi, k_aug_fn, mask_fn, kt_rows, zero_masked):
    r = q_aug.shape[1]

    def body(kt, carry):
        m, l, acc = carry
        off = pl.multiple_of(kt * kt_rows, kt_rows)
        k = k_aug_fn(kt, k_ref[0, pl.ds(off, kt_rows), :].astype(BF16))
        vt = v_ref[0, pl.ds(off, kt_rows), :].astype(F32).T.astype(BF16)
        s = jnp.dot(k, q_aug, preferred_element_type=F32)
        mask = mask_fn(kt)
        s = jnp.where(mask, s, NEG_INF)
        m_new = jnp.maximum(m, jnp.max(s, axis=0, keepdims=True))
        p = jnp.exp(s - m_new)
        if zero_masked:
            p = jnp.where(mask, p, 0.0)
        alpha = jnp.exp(m - m_new)
        l = alpha * l + jnp.sum(p, axis=0, keepdims=True)
        acc = alpha * acc + jnp.dot(vt, p.astype(BF16), preferred_element_type=F32)
        return m_new, l, acc

    init = (jnp.full((1, r), NEG_INF, F32), jnp.zeros((1, r), F32), jnp.zeros((LANES, r), F32))
    _, l, acc = lax.fori_loop(lo, hi, body, init)
    return acc / jnp.maximum(l, 1e-30)


def _nsa_kernel(*refs, **static):
    _nsa_body(pl.program_id(1), *refs, **static)


def _nsa_paged_kernel(pt_ref, qt_ref, gt_ref, kc_ref, vc_ref, cover_ref, cache_ref, tail_ref, kw_ref, vw_ref,
                      o_ref, kbuf_ref, vbuf_ref, sem_ref, **static):
    rows = _PagedRows(pt_ref, cache_ref, tail_ref, (kbuf_ref, vbuf_ref), sem_ref, 2)
    slot = rows.fetch()
    _nsa_body(0, qt_ref, gt_ref, kc_ref, vc_ref, cover_ref, rows.half(slot, 0), rows.half(slot, 1), kw_ref, vw_ref,
              o_ref, **static)


def _nsa_body(qi, qt_ref, gt_ref, kc_ref, vc_ref, cover_ref, ks_ref, vs_ref, kw_ref, vw_ref, o_ref,
              *, tq, q_start, kw_start, n_top, kt_rows):
    r = NSA_HEADS * tq
    t0 = q_start + qi * tq
    qt = qt_ref[0, 0]
    t_col = t0 + (lax.broadcasted_iota(jnp.int32, (1, r), 1) & (tq - 1))

    kc = kc_ref[0].astype(BF16)
    nch = kc.shape[0]
    s_c = jnp.dot(kc, qt, preferred_element_type=F32)
    cmp_end = lax.broadcasted_iota(jnp.int32, (nch, 1), 0) * CMP_STRIDE + (CMP_BLOCK - 1)
    p_c = _masked_softmax0(s_c, cmp_end <= t_col)
    o_c = jnp.dot(vc_ref[0].T.astype(BF16), p_c.astype(BF16), preferred_element_type=F32)

    ns_pad = cover_ref.shape[0]
    aligned = tq % LANES == 0
    lane_r = lax.broadcasted_iota(jnp.int32, (1, r), 1)
    cur = ((t0 + lax.broadcasted_iota(jnp.int32, (1, tq), 1)) if aligned else t_col) >> 6
    jb = lax.broadcasted_iota(jnp.int32, (ns_pad, 1), 0)
    jbf = jb.astype(F32)
    elig = jb <= cur
    forced = (jb == 0) | (jb == cur) | (jb == cur - 1)
    notsel = []
    for g in range(NSA_KV_HEADS):
        base = g * GROUP * tq
        if aligned:
            pg = p_c[:, base:base + tq]
            for n in range(1, GROUP):
                pg = pg + p_c[:, base + n * tq:base + (n + 1) * tq]
        else:
            pg = p_c
            for n in range(1, GROUP):
                pg = pg + pltpu.roll(p_c, r - n * tq, 1)
        imp = jnp.dot(cover_ref[...], pg, precision=HI, preferred_element_type=F32)
        score = jnp.where(elig, jnp.where(forced, SEL_FORCE, imp), -SEL_FORCE)
        sel = jnp.zeros(score.shape, F32)
        for _ in range(n_top):
            m = jnp.max(score, axis=0, keepdims=True)
            first = jnp.min(jnp.where(score == m, jbf, 1e9), axis=0, keepdims=True)
            hit = jbf == first
            sel = jnp.where(hit, 1.0, sel)
            score = jnp.where(hit, -3e38, score)
        ns = jnp.where(elig, 1.0 - sel, 1.0)
        if not aligned:
            ns = jnp.where((lane_r >= base) & (lane_r < base + tq), ns, 0.0)
            rep = ns
            for n in range(1, GROUP):
                rep = rep + pltpu.roll(ns, n * tq, 1)
            ns = rep
        notsel.append(ns)
    if aligned:
        notsel_cols = jnp.concatenate([notsel[0]] * GROUP + [notsel[1]] * GROUP, axis=1).astype(BF16)
    else:
        notsel_cols = (notsel[0] + notsel[1]).astype(BF16)

    q_aug = jnp.concatenate([qt, notsel_cols], axis=0)

    def sel_keys(kt, k):
        blk = kt * (kt_rows // SEL_BLOCK) + (lax.broadcasted_iota(jnp.int32, (kt_rows, ns_pad), 0) >> 6)
        tag = jnp.where(lax.broadcasted_iota(jnp.int32, (kt_rows, ns_pad), 1) == blk, -SEL_MASK_BIG, 0.0)
        return jnp.concatenate([k, tag.astype(BF16)], axis=1)

    def sel_mask(kt):
        kpos = kt * kt_rows + lax.broadcasted_iota(jnp.int32, (kt_rows, 1), 0)
        return kpos <= t_col

    hi_s = (t0 + tq - 1) // kt_rows + 1
    o_s = _flash_t(q_aug, ks_ref, vs_ref, 0, hi_s, sel_keys, sel_mask, kt_rows, zero_masked=False)

    def win_mask(kt):
        kpos = kw_start + kt * kt_rows + lax.broadcasted_iota(jnp.int32, (kt_rows, 1), 0)
        dist = t_col - kpos
        return (dist >= 0) & (dist <= WINDOW)

    n_wt = kw_ref.shape[1] // kt_rows
    lo_w = jnp.maximum(t0 - WINDOW - kw_start, 0) // kt_rows
    hi_w = jnp.minimum((t0 + tq - 1 - kw_start) // kt_rows + 1, n_wt)
    o_w = _flash_t(qt, kw_ref, vw_ref, lo_w, hi_w, lambda kt, k: k, win_mask, kt_rows, zero_masked=True)

    gt = gt_ref[0, 0]
    out = gt[0:1] * o_c + gt[1:2] * o_s + gt[2:3] * o_w
    o_ref[0, 0] = out.astype(o_ref.dtype)


def _cover_matrix_t(nch, ns_pad):
    ci = np.arange(nch)[None, :] * CMP_STRIDE
    sj = np.arange(ns_pad)[:, None] * SEL_BLOCK
    return jnp.asarray(((ci < sj + SEL_BLOCK) & (ci + CMP_BLOCK > sj)).astype(np.float32))


def _block_diag_qt(q, b, sq, tq):
    nqt = sq // tq
    qr = q.reshape(b, nqt, tq, NSA_KV_HEADS, GROUP, HEAD_DIM).transpose(0, 1, 3, 5, 4, 2)
    z = jnp.zeros_like(qr[:, :, 0])
    top = jnp.stack([qr[:, :, 0], z], axis=3)
    bot = jnp.stack([z, qr[:, :, 1]], axis=3)
    return jnp.stack([top, bot], axis=2).reshape(b, nqt, LANES, NSA_HEADS * tq)


def _nsa_specs(nch, ns_pad, lw, tq, kw_col, idx):
    r = NSA_HEADS * tq
    return dict(
        qt=pl.BlockSpec((1, 1, LANES, r), idx(lambda i, j: (i, j, 0, 0))),
        gt=pl.BlockSpec((1, 1, GATE_ROWS, r), idx(lambda i, j: (i, j, 0, 0))),
        kc=pl.BlockSpec((1, nch, LANES), idx(lambda i, j: (i, 0, 0))),
        cover=pl.BlockSpec((ns_pad, nch), idx(lambda i, j: (0, 0))),
        kw=pl.BlockSpec((1, lw, LANES), idx(lambda i, j: (i, 0, kw_col))),
        vw=pl.BlockSpec((1, lw, LANES), idx(lambda i, j: (i, 0, kw_col + 1))),
        out=pl.BlockSpec((1, 1, LANES, r), idx(lambda i, j: (i, j, 0, 0))))


def _gate_cols(gn, b, sq, tq):
    nqt = sq // tq
    g = gn[:, :3 * NSA_HEADS].reshape(b, nqt, tq, 3, NSA_HEADS).transpose(0, 1, 3, 4, 2)
    return jnp.pad(g.reshape(b, nqt, 3, NSA_HEADS * tq), ((0, 0), (0, 0), (0, GATE_ROWS - 3), (0, 0)))


def _own_head_rows(o, b, sq, tq):
    nqt = sq // tq
    o = o.reshape(b, nqt, NSA_KV_HEADS, HEAD_DIM, NSA_KV_HEADS, GROUP, tq)
    own = jnp.stack([o[:, :, g, :, g] for g in range(NSA_KV_HEADS)], axis=2)
    return own.transpose(0, 1, 5, 2, 4, 3).reshape(b * sq, NSA_WIDTH)


def _key_tile(lk, lw, most):
    for c in range(most, 0, -1):
        if lk % (c * KEY_TILE) == 0 and lw % (c * KEY_TILE) == 0:
            return c * KEY_TILE
    raise ValueError("row counts must be multiples of the key tile")


def _nsa(qt, gt, kc, vc, ksv, ks_col, kwv, kw_col, *, tq, q_start, kw_start, ns_true):
    b, nqt = qt.shape[:2]
    nch = kc.shape[1]
    lk = ksv.shape[1]
    ns_pad = _round_up(lk // SEL_BLOCK, LANES)
    sp = _nsa_specs(nch, ns_pad, kwv.shape[1], tq, kw_col, lambda f: f)
    kern = functools.partial(_nsa_kernel, tq=tq, q_start=q_start, kw_start=kw_start, n_top=min(SEL_TOP, ns_true),
                             kt_rows=_key_tile(lk, kwv.shape[1], 2))
    return pl.pallas_call(
        kern,
        grid=(b, nqt),
        in_specs=[sp["qt"], sp["gt"], sp["kc"], sp["kc"], sp["cover"],
                  pl.BlockSpec((1, lk, LANES), lambda i, j: (i, 0, ks_col)),
                  pl.BlockSpec((1, lk, LANES), lambda i, j: (i, 0, ks_col + 1)),
                  sp["kw"], sp["vw"]],
        out_specs=sp["out"],
        out_shape=jax.ShapeDtypeStruct((b, nqt, LANES, NSA_HEADS * tq), BF16),
        compiler_params=_cparams(("parallel", "arbitrary")),
        name="nsa",
    )(qt, gt, kc, vc, _cover_matrix_t(nch, ns_pad), ksv, ksv, kwv, kwv)


def _nsa_paged(qt, gt, kc, vc, page_table, cache2d, tail, kwv, *, tq, q_start, kw_start, ns_true):
    db, nqt = qt.shape[:2]
    assert nqt == 1
    nch = kc.shape[1]
    lk = (page_table.shape[1] + 1) * PAGE
    ns_pad = _round_up(lk // SEL_BLOCK, LANES)
    sp = _nsa_specs(nch, ns_pad, kwv.shape[1], tq, 0, lambda f: (lambda i, pt: f(i, 0)))
    kern = functools.partial(_nsa_paged_kernel, tq=tq, q_start=q_start, kw_start=kw_start, n_top=min(SEL_TOP, ns_true),
                             kt_rows=_key_tile(lk, kwv.shape[1], 5))
    any_spec = pl.BlockSpec(memory_space=pl.ANY)
    return pl.pallas_call(
        kern,
        grid_spec=pltpu.PrefetchScalarGridSpec(
            num_scalar_prefetch=1, grid=(db,),
            in_specs=[sp["qt"], sp["gt"], sp["kc"], sp["kc"], sp["cover"], any_spec, any_spec, sp["kw"], sp["vw"]],
            out_specs=sp["out"],
            scratch_shapes=[pltpu.VMEM((2, lk, KV_LANES), cache2d.dtype)] * 2 + [pltpu.SemaphoreType.DMA((2,))]),
        out_shape=jax.ShapeDtypeStruct((db, 1, LANES, NSA_HEADS * tq), BF16),
        compiler_params=_cparams(("arbitrary",)),
        name="nsa_paged",
    )(page_table, qt, gt, kc, vc, _cover_matrix_t(nch, ns_pad), cache2d, tail, kwv, kwv)


def _rwkv_pre_kernel(c_ref, p_ref, mu_ref, w0_ref, a0_ref, kkw_ref, ka_ref, wl_ref, g2_ref,
                     r_ref, k_ref, v_ref, kk_ref, a_ref, d_ref, g_ref):
    cols = c_ref[...]
    xs = cols + mu_ref[...] * (p_ref[...] - cols)
    o1, o2, o3 = RW_WIDTH, 2 * RW_WIDTH, 3 * RW_WIDTH
    o5 = o3 + RW_DECAY_LORA + RW_AAA_LORA
    k = xs[:, o1:o2]
    wa = xs[:, o3:o5]
    lane = lax.broadcasted_iota(jnp.int32, wa.shape, 1)
    wa = jnp.where(lane < RW_DECAY_LORA, jnp.tanh(wa), wa)
    lo = jnp.dot(wa.astype(BF16), wl_ref[...], preferred_element_type=F32)
    z = -(w0_ref[...] + lo[:, :RW_WIDTH])
    softplus = jnp.maximum(z, 0.0) + jnp.log(1.0 + jnp.exp(-jnp.abs(z)))
    w = -softplus - 0.5
    a = _sigmoid(a0_ref[...] + lo[:, RW_WIDTH:])
    r_ref[...] = xs[:, :o1]
    k_ref[...] = k * (1.0 + (a - 1.0) * ka_ref[...])
    v_ref[...] = xs[:, o2:o3]
    kk_ref[...] = k * kkw_ref[...]
    a_ref[...] = a
    d_ref[...] = jnp.exp(-jnp.exp(w))
    g_ref[...] = jnp.dot(_sigmoid(xs[:, o5:]).astype(BF16), g2_ref[...], preferred_element_type=F32)


def _rwkv_pre(cols, prev, row0, nrows, mu, w0, a0, kkw, ka, wlora, g2):
    tm = TOK_TILE
    assert row0 % tm == 0 and nrows % tm == 0 and row0 + nrows <= cols.shape[0]
    tile0 = row0 // tm
    row = lambda i: (i, 0)
    src = lambda i: (i + tile0, 0)
    const = lambda i: (0, 0)
    vec = pl.BlockSpec((1, RW_WIDTH), const)
    out = jax.ShapeDtypeStruct((nrows, RW_WIDTH), F32)
    return pl.pallas_call(
        _rwkv_pre_kernel,
        grid=(nrows // tm,),
        in_specs=[pl.BlockSpec((tm, RW_COLS), src), pl.BlockSpec((tm, RW_COLS), src),
                  pl.BlockSpec((1, RW_COLS), const), vec, vec, vec, vec,
                  pl.BlockSpec((RW_DECAY_LORA + RW_AAA_LORA, 2 * RW_WIDTH), const),
                  pl.BlockSpec((RW_GATE_LORA, RW_WIDTH), const)],
        out_specs=[pl.BlockSpec((tm, RW_WIDTH), row)] * 7,
        out_shape=(out,) * 7,
        compiler_params=_cparams(("parallel",)),
        name="rwkv_pre",
    )(cols, prev, mu, w0, a0, kkw, ka, wlora, g2)


def _rwkv_scan_kernel(r_ref, k_ref, v_ref, kk_ref, a_ref, d_ref, g_ref, s0_ref, lnw_ref, lnb_ref, rk_ref,
                      o_ref, sf_ref, st_ref, kkn_ref, b_ref):
    n = RW_HEAD_DIM
    tb = r_ref.shape[0]
    ti = pl.program_id(1)

    @pl.when(ti == 0)
    def _():
        st_ref[...] = s0_ref[...]

    def step(t, c):
        kk = kk_ref[t]
        nrm = jnp.sqrt(jnp.sum(kk * kk, axis=0, keepdims=True))
        kkn = kk / jnp.maximum(nrm, 1e-12)
        kkn_ref[...] = kkn
        b_ref[...] = kkn * a_ref[t]
        vv = v_ref[t]
        sa = jnp.zeros((n, LANES), F32)
        for i in range(n):
            sa = sa - st_ref[i] * kkn_ref[pl.ds(i, 1), :]
        y = jnp.zeros((n, LANES), F32)
        for i in range(n):
            s_new = (st_ref[i] * d_ref[t, pl.ds(i, 1), :] + sa * b_ref[pl.ds(i, 1), :]
                     + vv * k_ref[t, pl.ds(i, 1), :])
            st_ref[i] = s_new
            y = y + s_new * r_ref[t, pl.ds(i, 1), :]
        mu = jnp.mean(y, axis=0, keepdims=True)
        yc = y - mu
        var = jnp.mean(yc * yc, axis=0, keepdims=True)
        yn = yc * lax.rsqrt(var + GN_EPS) * lnw_ref[...] + lnb_ref[...]
        bonus = jnp.sum(r_ref[t] * k_ref[t] * rk_ref[...], axis=0, keepdims=True) * vv
        o_ref[t] = (yn + bonus) * g_ref[t]
        return c

    lax.fori_loop(0, tb, step, 0)

    @pl.when(ti == pl.num_programs(1) - 1)
    def _():
        sf_ref[...] = st_ref[...]


def _rwkv_scan(seqs, s0, lnw, lnb, rk, tb):
    s, n, nl = seqs[0].shape
    blk = pl.BlockSpec((tb, n, LANES), lambda j, i: (i, 0, j))
    st = pl.BlockSpec((n, n, LANES), lambda j, i: (0, 0, j))
    vec = pl.BlockSpec((n, LANES), lambda j, i: (0, j))
    return pl.pallas_call(
        _rwkv_scan_kernel,
        grid=(nl // LANES, s // tb),
        in_specs=[blk] * 7 + [st, vec, vec, vec],
        out_specs=[blk, st],
        out_shape=(jax.ShapeDtypeStruct((s, n, nl), F32), jax.ShapeDtypeStruct((n, n, nl), F32)),
        scratch_shapes=[pltpu.VMEM((n, n, LANES), F32), pltpu.VMEM((n, LANES), F32), pltpu.VMEM((n, LANES), F32)],
        compiler_params=_cparams(("parallel", "arbitrary")),
        name="rwkv_scan",
    )(*seqs, s0, lnw, lnb, rk)


def _to_scan_layout(x, b, s):
    return x.reshape(b, s, RW_HEADS, RW_HEAD_DIM).transpose(1, 3, 0, 2).reshape(s, RW_HEAD_DIM, b * RW_HEADS)


def _from_scan_layout(y, b, s):
    return y.reshape(s, RW_HEAD_DIM, b, RW_HEADS).transpose(2, 0, 3, 1).reshape(b * s, RW_WIDTH)


def _rwkv_group(pre, b, s, wkv0, lnw, lnb, rk):
    nl = b * RW_HEADS
    nlp = _round_up(nl, LANES)

    def padl(x):
        return x if nlp == nl else jnp.pad(x, [(0, 0)] * (x.ndim - 1) + [(0, nlp - nl)])

    seqs = [padl(_to_scan_layout(x, b, s)) for x in pre]
    s0 = padl(wkv0.astype(F32).transpose(3, 2, 0, 1).reshape(RW_HEAD_DIM, RW_HEAD_DIM, nl))
    tile = lambda p: padl(jnp.tile(p.reshape(RW_HEADS, RW_HEAD_DIM).T, (1, b)))
    tb = math.gcd(s, 32)
    o, sf = _rwkv_scan(seqs, s0, tile(lnw), tile(lnb), tile(rk), tb)
    o = _from_scan_layout(o[:, :, :nl], b, s)
    sf = sf[:, :, :nl].reshape(RW_HEAD_DIM, RW_HEAD_DIM, b, RW_HEADS).transpose(2, 3, 1, 0)
    return o, sf


def _merge_kernel(x_ref, on_ref, or_ref, mg_ref, wpa_ref, wpb_ref, wo_ref, n2_ref, wr_ref, br_ref,
                  h_ref, xn_ref, te_ref, tg_ref):
    pa = jnp.dot(on_ref[...], wpa_ref[...], preferred_element_type=F32)
    pb = jnp.dot(or_ref[...].astype(BF16), wpb_ref[...], preferred_element_type=F32)
    merged = mg_ref[:, :D_MODEL] * pa + mg_ref[:, D_MODEL:] * pb
    h = x_ref[...] + jnp.dot(merged.astype(BF16), wo_ref[...], preferred_element_type=F32)
    h_ref[...] = h
    ms = jnp.mean(h * h, axis=-1, keepdims=True)
    xn = h * lax.rsqrt(ms + RMS_EPS) * n2_ref[...]
    xn_ref[...] = xn
    logits = jnp.dot(xn, wr_ref[...], precision=HI, preferred_element_type=F32) + br_ref[...]
    lane = lax.broadcasted_iota(jnp.int32, logits.shape, 1)
    lanef = lane.astype(F32)
    te = jnp.zeros(logits.shape, F32)
    tv = jnp.full(logits.shape, NEG_INF, F32)
    for kx in range(TOP_K):
        m = jnp.max(logits, axis=-1, keepdims=True)
        first = jnp.min(jnp.where(logits == m, lanef, 1e9), axis=-1, keepdims=True)
        te = jnp.where(lane == kx, first, te)
        tv = jnp.where(lane == kx, m, tv)
        logits = jnp.where(lanef == first, -3e38, logits)
    e = jnp.where(lane < TOP_K, jnp.exp(tv - jnp.max(tv, axis=-1, keepdims=True)), 0.0)
    te_ref[...] = te.astype(jnp.int32)
    tg_ref[...] = e / jnp.sum(e, axis=-1, keepdims=True)


def _merge(x, o_nsa, o_rw, mg, wpa, wpb, wo, norm2, wr_pad, br_pad):
    t = x.shape[0]
    tm = TOK_TILE
    row = lambda i: (i, 0)
    const = lambda i: (0, 0)
    outs = (jax.ShapeDtypeStruct((t, D_MODEL), F32), jax.ShapeDtypeStruct((t, D_MODEL), F32),
            jax.ShapeDtypeStruct((t, LANES), jnp.int32), jax.ShapeDtypeStruct((t, LANES), F32))
    return pl.pallas_call(
        _merge_kernel,
        grid=(t // tm,),
        in_specs=[pl.BlockSpec((tm, D_MODEL), row), pl.BlockSpec((tm, NSA_WIDTH), row),
                  pl.BlockSpec((tm, RW_WIDTH), row), pl.BlockSpec((tm, 2 * D_MODEL), row),
                  pl.BlockSpec((NSA_WIDTH, D_MODEL), const), pl.BlockSpec((RW_WIDTH, D_MODEL), const),
                  pl.BlockSpec((D_MODEL, D_MODEL), const), pl.BlockSpec((1, D_MODEL), const),
                  pl.BlockSpec((D_MODEL, LANES), const), pl.BlockSpec((1, LANES), const)],
        out_specs=[pl.BlockSpec((tm, o.shape[1]), row) for o in outs],
        out_shape=outs,
        compiler_params=_cparams(("parallel",)),
        name="merge",
    )(x, o_nsa, o_rw, mg, wpa, wpb, wo, norm2.reshape(1, D_MODEL), wr_pad, br_pad)


class _GatheredRows:
    def __init__(self, src_ref, buf_ref, sem_ref):
        self.src, self.buf, self.sem = src_ref, buf_ref, sem_ref
        self.n = buf_ref.shape[1]

    def start(self, idx_ref, off, slot):
        def body(j, c):
            pltpu.make_async_copy(self.src.at[pl.ds(idx_ref[off + j], 1)], self.buf.at[slot, pl.ds(j, 1)],
                                  self.sem.at[slot]).start()
            return c
        lax.fori_loop(0, self.n, body, 0, unroll=8)

    def wait(self, slot):
        pltpu.make_async_copy(self.src.at[pl.ds(0, self.n)], self.buf.at[slot], self.sem.at[slot]).wait()


def _expert_kernel(be_ref, nu_ref, idx_ref, idxn_ref, xn_ref, wu_ref, bu_ref, wd_ref, bd_ref, o_ref, xbuf_ref, sem_ref):
    i = pl.program_id(0)
    n_used = nu_ref[0]
    slot = i & 1
    bm = xbuf_ref.shape[1]
    rows = _GatheredRows(xn_ref, xbuf_ref, sem_ref)

    @pl.when(i == 0)
    def _():
        rows.start(idx_ref, 0, slot)

    @pl.when(i + 1 < n_used)
    def _():
        rows.start(idxn_ref, (1 - slot) * bm, 1 - slot)

    @pl.when(i < n_used)
    def _():
        rows.wait(slot)
        x = xbuf_ref[slot].astype(BF16)
        hu = jnp.dot(x, wu_ref[0], preferred_element_type=F32) + bu_ref[0]
        gt = jnp.minimum(hu[:, :D_FF], SWIGLU_LIMIT)
        up = jnp.clip(hu[:, D_FF:], -SWIGLU_LIMIT, SWIGLU_LIMIT)
        hh = (up + 1.0) * gt * _sigmoid(SWIGLU_ALPHA * gt)
        o_ref[...] = jnp.dot(hh.astype(BF16), wd_ref[0], preferred_element_type=F32) + bd_ref[0]

    @pl.when(i >= n_used)
    def _():
        o_ref[...] = jnp.zeros(o_ref.shape, o_ref.dtype)


def _experts(blk_exp, n_used, slot_tok, xn, w_up, b_up, w_down, b_down):
    p = slot_tok.shape[0]
    bm = MOE_BM
    nb = p // bm
    assert nb % 2 == 0
    wmap = lambda i, be, nu: (be[i], 0, 0)
    return pl.pallas_call(
        _expert_kernel,
        grid_spec=pltpu.PrefetchScalarGridSpec(
            num_scalar_prefetch=2, grid=(nb,),
            in_specs=[pl.BlockSpec((2 * bm,), lambda i, be, nu: (i // 2,), memory_space=pltpu.SMEM),
                      pl.BlockSpec((2 * bm,), lambda i, be, nu: (jnp.minimum(i + 1, nb - 1) // 2,),
                                   memory_space=pltpu.SMEM),
                      pl.BlockSpec(memory_space=pl.ANY),
                      pl.BlockSpec((1, D_MODEL, 2 * D_FF), wmap), pl.BlockSpec((1, 1, 2 * D_FF), wmap),
                      pl.BlockSpec((1, D_FF, D_MODEL), wmap), pl.BlockSpec((1, 1, D_MODEL), wmap)],
            out_specs=pl.BlockSpec((bm, D_MODEL), lambda i, be, nu: (i, 0)),
            scratch_shapes=[pltpu.VMEM((2, bm, D_MODEL), xn.dtype), pltpu.SemaphoreType.DMA((2,))]),
        out_shape=jax.ShapeDtypeStruct((p, D_MODEL), F32),
        compiler_params=_cparams(("arbitrary",)),
        name="experts",
    )(blk_exp, n_used, slot_tok, slot_tok, xn, w_up, b_up.reshape(N_EXPERTS, 1, 2 * D_FF), w_down,
      b_down.reshape(N_EXPERTS, 1, D_MODEL))


def _final_kernel(idx_ref, idxn_ref, h_ref, yb_ref, tg_ref, nf_ref, o_ref, ybuf_ref, sem_ref):
    i = pl.program_id(0)
    slot = i & 1
    tm = h_ref.shape[0]
    rows = _GatheredRows(yb_ref, ybuf_ref, sem_ref)

    @pl.when(i == 0)
    def _():
        rows.start(idx_ref, 0, slot)

    @pl.when(i + 1 < pl.num_programs(0))
    def _():
        rows.start(idxn_ref, 0, 1 - slot)

    rows.wait(slot)
    tg = tg_ref[...]
    f = tg[:, 0:1] * ybuf_ref[slot, pl.ds(0, tm), :]
    for kx in range(1, TOP_K):
        f = f + tg[:, kx:kx + 1] * ybuf_ref[slot, pl.ds(kx * tm, tm), :]
    y = h_ref[...] + f
    ms = jnp.mean(y * y, axis=-1, keepdims=True)
    o_ref[...] = y * lax.rsqrt(ms + RMS_EPS) * nf_ref[...]


def _final(h, yb, slot_of, tg, norm_f):
    t = h.shape[0]
    tm = TOK_TILE
    nt = t // tm
    idx = slot_of.reshape(nt, tm, TOP_K).transpose(0, 2, 1).reshape(nt * TOP_K * tm)
    row = lambda i: (i, 0)
    return pl.pallas_call(
        _final_kernel,
        grid=(nt,),
        in_specs=[pl.BlockSpec((TOP_K * tm,), lambda i: (i,), memory_space=pltpu.SMEM),
                  pl.BlockSpec((TOP_K * tm,), lambda i: (jnp.minimum(i + 1, nt - 1),), memory_space=pltpu.SMEM),
                  pl.BlockSpec((tm, D_MODEL), row), pl.BlockSpec(memory_space=pl.ANY),
                  pl.BlockSpec((tm, LANES), row), pl.BlockSpec((1, D_MODEL), lambda i: (0, 0))],
        out_specs=pl.BlockSpec((tm, D_MODEL), row),
        out_shape=jax.ShapeDtypeStruct((t, D_MODEL), F32),
        scratch_shapes=[pltpu.VMEM((2, TOP_K * tm, D_MODEL), yb.dtype), pltpu.SemaphoreType.DMA((2,))],
        compiler_params=_cparams(("arbitrary",)),
        name="final",
    )(idx, idx, h, yb, tg, norm_f.reshape(1, D_MODEL))


def _routing(top_e, bm):
    t = top_e.shape[0]
    a = t * TOP_K
    e_flat = top_e.reshape(a)
    order = jnp.argsort(e_flat).astype(jnp.int32)
    e_sorted = e_flat[order]
    counts = jnp.bincount(e_flat, length=N_EXPERTS).astype(jnp.int32)
    starts = jnp.cumsum(counts) - counts
    padded = (counts + bm - 1) // bm * bm
    pends = jnp.cumsum(padded)
    pstarts = pends - padded
    slot_sorted = (pstarts[e_sorted] + jnp.arange(a, dtype=jnp.int32) - starts[e_sorted]).astype(jnp.int32)
    n_blocks = _round_up(-(-a // bm) + N_EXPERTS, IDX_BLOCKS)
    p = n_blocks * bm
    blk_exp = jnp.minimum(jnp.sum(jnp.arange(n_blocks)[:, None] * bm >= pends[None, :], axis=1),
                          N_EXPERTS - 1).astype(jnp.int32)
    slot = jnp.arange(p, dtype=jnp.int32)
    e_slot = jnp.repeat(blk_exp, bm)
    j = slot - pstarts[e_slot]
    src = jnp.clip(starts[e_slot] + j, 0, a - 1)
    slot_tok = jnp.where((j < counts[e_slot]) & (slot < pends[-1]), order[src] // TOP_K, 0).astype(jnp.int32)
    slot_of = slot_sorted[jnp.argsort(order)].reshape(t, TOP_K)
    n_used = (pends[-1:] // bm).astype(jnp.int32)
    return slot_tok, slot_of, blk_exp, n_used


def kernel(x_prompt, x_sample, cache_kv, cache_win, state_wkv, state_shift, page_table, norm1, w_in, cmp_pe, cmp_mix, cmp_w1, cmp_w2, rw_mu, rw_w0, rw_w2, rw_a0, rw_a2, rw_g2, rw_kk, rw_ka, rw_rk, rw_ln_w, rw_ln_b, w_pa, w_pb, w_o, norm2, w_router, b_router, w_up, b_up, w_down, b_down, norm_f):
    assert w_in.shape[0] == 1, "single-layer trunk"
    b, s, _ = x_prompt.shape
    db, ds, _ = x_sample.shape
    n_pages = page_table.shape[1]
    assert cache_kv.shape[2] == PAGE
    past = n_pages * PAGE
    nbuf = cache_win.shape[2]
    tp, ts = b * s, db * ds
    t_real = tp + ts
    t_pad = _round_up(t_real, TOK_ALIGN)

    wi = w_in[0]
    w_pad = jnp.concatenate(
        [wi[:, :OFF_GN], jnp.pad(wi[:, OFF_GN:OFF_RW], ((0, 0), (0, GN_PAD - 3 * NSA_HEADS))), wi[:, OFF_RW:]],
        axis=1).astype(BF16)
    tile_g = lambda p: jnp.tile(p, (1, 1, NSA_KV_HEADS))
    mix2, pe2 = tile_g(cmp_mix[0].astype(F32)), tile_g(cmp_pe[0].astype(F32))
    bd = lambda w: jnp.stack([jnp.kron(jnp.eye(NSA_KV_HEADS, dtype=F32), w[i].astype(F32)) for i in range(2)])
    w1bd, w2bd = bd(cmp_w1[0]), bd(cmp_w2[0])
    zl = jnp.zeros((RW_DECAY_LORA, RW_WIDTH), F32)
    wlora = jnp.concatenate([jnp.concatenate([rw_w2[0], zl], axis=1),
                             jnp.concatenate([zl, rw_a2[0]], axis=1)], axis=0).astype(BF16)
    vec = lambda p: p[0].reshape(1, -1).astype(F32)
    wr_pad = jnp.pad(w_router[0].astype(F32), ((0, 0), (0, LANES - N_EXPERTS)))
    br_pad = jnp.concatenate([b_router[0].astype(F32), jnp.full((LANES - N_EXPERTS,), NEG_INF, F32)]).reshape(1, LANES)

    x_all = jnp.concatenate([x_prompt.reshape(tp, D_MODEL), x_sample.reshape(ts, D_MODEL)], axis=0)
    x_all = jnp.pad(x_all, ((0, t_pad - t_real), (0, 0)))
    q_all, kv_all, kvb_all, gn_all, rw_all, mg_all = _proj(x_all, norm1[0], w_pad)

    kv_p = kv_all[:tp].reshape(b, s, 6 * KV_LANES)
    kv_s = kv_all[tp:t_real].reshape(db, ds, 6 * KV_LANES)
    kvb_p = kvb_all[:tp].reshape(b, s, 6 * KV_LANES)

    tq = NSA_TQ
    assert s % tq == 0 and ds <= PAGE
    kc_p, vc_p = _compress(kv_p, mix2, pe2, w1bd, w2bd)
    o_nsa_p = _nsa(_block_diag_qt(q_all[:tp], b, s, tq), _gate_cols(gn_all[:tp], b, s, tq), kc_p, vc_p,
                   kvb_p, 2, kvb_p, 4, tq=tq, q_start=0, kw_start=0, ns_true=s // SEL_BLOCK)

    row_w = 4 * KV_LANES
    tail = jnp.pad(kv_s[:, :, :row_w], ((0, 0), (0, PAGE - ds), (0, 0)))
    cache2d = cache_kv[0].reshape(-1, PAGE, row_w)
    kc_s, vc_s = _compress_paged(page_table, cache2d, tail, mix2, pe2, w1bd, w2bd)
    kw_full = jnp.concatenate([cache_win[0].reshape(db, nbuf, 2 * KV_LANES), kv_s[:, :, row_w:]], axis=1)
    lw = _round_up(nbuf + ds, KEY_TILE)
    kw_pad = jnp.pad(kw_full, ((0, 0), (0, lw - (nbuf + ds)), (0, 0)))
    lp = _round_up(past + ds, SEL_BLOCK)
    tqs = NSA_TQ_FEW if ds <= NSA_TQ_FEW else NSA_TQ
    assert ds <= tqs
    pad_q = lambda x: jnp.pad(x.reshape(db, ds, -1), ((0, 0), (0, tqs - ds), (0, 0))).reshape(db * tqs, -1)
    o_nsa_s = _nsa_paged(_block_diag_qt(pad_q(q_all[tp:t_real]), db, tqs, tqs),
                         _gate_cols(pad_q(gn_all[tp:t_real]), db, tqs, tqs), kc_s, vc_s, page_table, cache2d, tail,
                         kw_pad, tq=tqs, q_start=past, kw_start=past - nbuf, ns_true=lp // SEL_BLOCK)
    o_nsa = jnp.concatenate([_own_head_rows(o_nsa_p, b, s, tq),
                             _own_head_rows(o_nsa_s, db, tqs, tqs).reshape(db, tqs, NSA_WIDTH)[:, :ds].reshape(ts, NSA_WIDTH),
                             jnp.zeros((t_pad - t_real, NSA_WIDTH), BF16)], axis=0)

    rw_p = rw_all[:tp].reshape(b, s, RW_COLS)
    rw_s = rw_all[tp:t_real].reshape(db, ds, RW_COLS)
    prev_p = jnp.concatenate([jnp.zeros((b, 1, RW_COLS), F32), rw_p[:, :-1]], axis=1)
    prev_s = jnp.concatenate([state_shift[0][:, None].astype(F32), rw_s[:, :-1]], axis=1)
    prev_all = jnp.concatenate([prev_p.reshape(tp, RW_COLS), prev_s.reshape(ts, RW_COLS),
                                jnp.zeros((t_pad - t_real, RW_COLS), F32)], axis=0)
    assert tp % TOK_ALIGN == 0
    pre_args = (vec(rw_mu), vec(rw_w0), vec(rw_a0), vec(rw_kk), vec(rw_ka), wlora, rw_g2[0].astype(BF16))
    pre_p = _rwkv_pre(rw_all, prev_all, 0, tp, *pre_args)
    pre_s = _rwkv_pre(rw_all, prev_all, tp, _round_up(ts, TOK_TILE), *pre_args)
    o_rw_p, wkv_p = _rwkv_group(pre_p, b, s, jnp.zeros((b, RW_HEADS, RW_HEAD_DIM, RW_HEAD_DIM), F32),
                                rw_ln_w[0], rw_ln_b[0], rw_rk[0])
    o_rw_s, wkv_s = _rwkv_group([x[:ts] for x in pre_s], db, ds, state_wkv[0],
                                rw_ln_w[0], rw_ln_b[0], rw_rk[0])
    o_rw = jnp.concatenate([o_rw_p, o_rw_s, jnp.zeros((t_pad - t_real, RW_WIDTH), F32)], axis=0)

    h, xn2, te, tg = _merge(x_all, o_nsa, o_rw, mg_all, w_pa[0].astype(BF16), w_pb[0].astype(BF16),
                            w_o[0].astype(BF16), norm2[0], wr_pad, br_pad)

    slot_tok, slot_of, blk_exp, n_used = _routing(te[:, :TOP_K], MOE_BM)
    yb = _experts(blk_exp, n_used, slot_tok, xn2, w_up[0].astype(BF16), b_up[0].astype(F32), w_down[0].astype(BF16),
                  b_down[0].astype(F32))
    y_all = _final(h, yb, slot_of, tg, norm_f)

    y_prompt = y_all[:tp].reshape(b, s, D_MODEL)
    y_sample = y_all[tp:t_real].reshape(db, ds, D_MODEL)
    kv_shape = (4, NSA_KV_HEADS, HEAD_DIM)
    win_shape = (2, NSA_KV_HEADS, HEAD_DIM)
    kv_prompt = kv_p[:, :, :row_w].reshape((1, b, s) + kv_shape)
    kv_sample = kv_s[:, :, :row_w].reshape((1, db, ds) + kv_shape)
    keep_p = min(WINDOW, s)
    win_prompt = kv_p[:, s - keep_p:, row_w:].reshape((1, b, keep_p) + win_shape)
    win_sample = kw_full[:, nbuf + ds - nbuf:].reshape((1, db, nbuf) + win_shape)
    shift_prompt = rw_p[:, -1][None]
    shift_sample = rw_s[:, -1][None]
    return (y_prompt, y_sample, kv_prompt, kv_sample, win_prompt, win_sample,
            wkv_p[None], wkv_s[None], shift_prompt, shift_sample)
```

```python
import functools
import math

import jax
import jax.numpy as jnp
import numpy as np
from jax import lax
from jax.experimental import pallas as pl
from jax.experimental.pallas import tpu as pltpu

F32 = jnp.float32
BF16 = jnp.bfloat16
HI = lax.Precision.HIGHEST

D_MODEL = 1024
NSA_HEADS = 8
NSA_KV_HEADS = 2
HEAD_DIM = 64
GROUP = NSA_HEADS // NSA_KV_HEADS
NSA_WIDTH = NSA_HEADS * HEAD_DIM
KV_LANES = NSA_KV_HEADS * HEAD_DIM
CMP_BLOCK = 32
CMP_STRIDE = 16
SEL_BLOCK = 64
SEL_TOP = 16
WINDOW = 512
SEL_FORCE = 1e9
NEG_INF = -1e30
SEL_MASK_BIG = 2.0 ** 100
PAGE = 128

RW_HEADS = 8
RW_HEAD_DIM = 64
RW_WIDTH = RW_HEADS * RW_HEAD_DIM
RW_DECAY_LORA = 64
RW_AAA_LORA = 64
RW_GATE_LORA = 128
RW_COLS = 3 * RW_WIDTH + RW_DECAY_LORA + RW_AAA_LORA + RW_GATE_LORA
GN_EPS = 64e-5

N_EXPERTS = 32
TOP_K = 4
D_FF = 1024
SWIGLU_LIMIT = 7.0
SWIGLU_ALPHA = 1.702
RMS_EPS = 1e-5

OFF_KV = NSA_WIDTH
OFF_GN = OFF_KV + 6 * KV_LANES
OFF_RW = OFF_GN + 3 * NSA_HEADS
OFF_MG = OFF_RW + RW_COLS
N_PROJ = OFF_MG + 2 * D_MODEL

LANES = 128
GN_PAD = LANES
P_Q = 0
P_KV = P_Q + NSA_WIDTH
P_GN = P_KV + 6 * KV_LANES
P_RW = P_GN + GN_PAD
P_MG = P_RW + RW_COLS
P_END = P_MG + 2 * D_MODEL

VMEM_LIMIT = 52 * 1024 * 1024

TOK_TILE = 256
TOK_ALIGN = 512
MOE_BM = 512
IDX_BLOCKS = 2
KEY_TILE = 128
NSA_TQ = 128
NSA_TQ_FEW = 16
GATE_ROWS = 8


def _cparams(sem):
    return pltpu.CompilerParams(dimension_semantics=sem, vmem_limit_bytes=VMEM_LIMIT)


def _sigmoid(x):
    return 1.0 / (1.0 + jnp.exp(-x))


def _round_up(a, b):
    return -(-a // b) * b


def _proj_kernel(x_ref, g_ref, w_ref, q_ref, kv_ref, kvb_ref, gn_ref, rw_ref, mg_ref):
    x = x_ref[...]
    ms = jnp.mean(x * x, axis=-1, keepdims=True)
    xn = (x * lax.rsqrt(ms + RMS_EPS) * g_ref[...]).astype(BF16)

    def seg(a, b):
        return jnp.dot(xn, w_ref[:, a:b], preferred_element_type=F32)

    q_ref[...] = (seg(P_Q, P_KV) * (HEAD_DIM ** -0.5)).astype(BF16)
    kv = seg(P_KV, P_GN)
    kv_ref[...] = kv
    kvb_ref[...] = kv.astype(BF16)
    gn_ref[...] = _sigmoid(seg(P_GN, P_RW))
    rw_ref[...] = seg(P_RW, P_MG)
    mg_ref[...] = _sigmoid(seg(P_MG, P_END))


def _proj(x, norm1, w_pad):
    T = x.shape[0]
    tm = TOK_TILE
    row = lambda i: (i, 0)
    const = lambda i: (0, 0)
    outs = (
        jax.ShapeDtypeStruct((T, NSA_WIDTH), BF16),
        jax.ShapeDtypeStruct((T, 6 * KV_LANES), F32),
        jax.ShapeDtypeStruct((T, 6 * KV_LANES), BF16),
        jax.ShapeDtypeStruct((T, GN_PAD), F32),
        jax.ShapeDtypeStruct((T, RW_COLS), F32),
        jax.ShapeDtypeStruct((T, 2 * D_MODEL), F32),
    )
    return pl.pallas_call(
        _proj_kernel,
        grid=(T // tm,),
        in_specs=[pl.BlockSpec((tm, D_MODEL), row), pl.BlockSpec((1, D_MODEL), const),
                  pl.BlockSpec((D_MODEL, P_END), const)],
        out_specs=[pl.BlockSpec((tm, o.shape[1]), row) for o in outs],
        out_shape=outs,
        compiler_params=_cparams(("parallel",)),
        name="proj",
    )(x, norm1.reshape(1, D_MODEL), w_pad)


class _PagedRows:
    def __init__(self, pt_ref, cache_ref, tail_ref, buf_refs, sem_ref, col0):
        self.pt, self.cache, self.tail, self.bufs, self.sem, self.col0 = pt_ref, cache_ref, tail_ref, buf_refs, sem_ref, col0
        self.n_pages = pt_ref.shape[1]

    def _copies(self, src, row0, slot):
        return [pltpu.make_async_copy(src.at[:, pl.ds((self.col0 + i) * KV_LANES, KV_LANES)],
                                      buf.at[slot, pl.ds(row0, PAGE), :], self.sem.at[slot])
                for i, buf in enumerate(self.bufs)]

    def _page(self, b, p, slot):
        return self._copies(self.cache.at[self.pt[b, p]], p * PAGE, slot)

    def _tail(self, b, slot):
        return self._copies(self.tail.at[b], self.n_pages * PAGE, slot)

    def start(self, b, slot):
        def body(p, c):
            for cp in self._page(b, p, slot):
                cp.start()
            return c
        lax.fori_loop(0, self.n_pages, body, 0)
        for cp in self._tail(b, slot):
            cp.start()

    def wait(self, b, slot):
        def body(p, c):
            for cp in self._page(b, p, slot):
                cp.wait()
            return c
        lax.fori_loop(0, self.n_pages, body, 0)
        for cp in self._tail(b, slot):
            cp.wait()

    def fetch(self):
        b = pl.program_id(0)
        slot = b & 1

        @pl.when(b == 0)
        def _():
            self.start(b, slot)

        @pl.when(b + 1 < pl.num_programs(0))
        def _():
            self.start(b + 1, 1 - slot)

        self.wait(b, slot)
        return slot

    def half(self, slot, i):
        return self.bufs[i].at[pl.ds(slot, 1)]


def _gelu_tanh(x):
    return 0.5 * x * (1.0 + jnp.tanh(math.sqrt(2.0 / math.pi) * (x + 0.044715 * (x * x * x))))


def _compress_paged_kernel(pt_ref, cache_ref, tail_ref, mix_ref, pe_ref, w1_ref, w2_ref, kc_ref, vc_ref,
                           kbuf_ref, vbuf_ref, sem_ref):
    rows = _PagedRows(pt_ref, cache_ref, tail_ref, (kbuf_ref, vbuf_ref), sem_ref, 0)
    slot = rows.fetch()
    _compress_kernel(rows.half(slot, 0), rows.half(slot, 1), mix_ref, pe_ref, w1_ref, w2_ref, kc_ref, vc_ref)


def _compress_paged(page_table, cache2d, tail, mix2, pe2, w1bd, w2bd):
    db, n_pages = page_table.shape
    l = (n_pages + 1) * PAGE
    nch = _round_up(l // CMP_STRIDE, LANES)
    const3 = lambda i, pt: (0, 0, 0)
    any_spec = pl.BlockSpec(memory_space=pl.ANY)
    out = jax.ShapeDtypeStruct((db, nch, LANES), F32)
    return pl.pallas_call(
        _compress_paged_kernel,
        grid_spec=pltpu.PrefetchScalarGridSpec(
            num_scalar_prefetch=1, grid=(db,),
            in_specs=[any_spec, any_spec,
                      pl.BlockSpec((2, CMP_BLOCK, LANES), const3), pl.BlockSpec((2, CMP_BLOCK, LANES), const3),
                      pl.BlockSpec((2, LANES, LANES), const3), pl.BlockSpec((2, LANES, LANES), const3)],
            out_specs=[pl.BlockSpec((1, nch, LANES), lambda i, pt: (i, 0, 0))] * 2,
            scratch_shapes=[pltpu.VMEM((2, l, KV_LANES), cache2d.dtype)] * 2 + [pltpu.SemaphoreType.DMA((2,))]),
        out_shape=(out, out),
        compiler_params=_cparams(("arbitrary",)),
        name="compress_paged",
    )(page_table, cache2d, tail, mix2, pe2, w1bd, w2bd)


def _compress_kernel(rk_ref, rv_ref, mix_ref, pe_ref, w1_ref, w2_ref, kc_ref, vc_ref):
    nch = rk_ref.shape[1] // CMP_STRIDE
    nch_pad = kc_ref.shape[1]
    for idx, (r_ref, o_ref) in enumerate(((rk_ref, kc_ref), (rv_ref, vc_ref))):
        mix = mix_ref[idx]
        cs0 = jnp.zeros((nch, LANES), F32)
        cs1 = jnp.zeros((nch, LANES), F32)
        for p in range(CMP_STRIDE):
            xp = r_ref[0, pl.ds(p, nch, stride=CMP_STRIDE), :].astype(F32)
            cs0 = cs0 + xp * mix[p:p + 1]
            cs1 = cs1 + xp * mix[CMP_STRIDE + p:CMP_STRIDE + p + 1]
        c0 = jnp.sum(mix * pe_ref[idx], axis=0, keepdims=True)
        pre = c0 + cs0 + pltpu.roll(cs1, nch - 1, 0)
        h = _gelu_tanh(jnp.dot(pre, w1_ref[idx], precision=HI, preferred_element_type=F32))
        o_ref[0, pl.ds(0, nch), :] = jnp.dot(h, w2_ref[idx], precision=HI, preferred_element_type=F32)
        if nch_pad > nch:
            o_ref[0, pl.ds(nch, nch_pad - nch), :] = jnp.zeros((nch_pad - nch, LANES), F32)


def _compress(rows3d, mix2, pe2, w1bd, w2bd):
    b, l, _ = rows3d.shape
    nch = _round_up(l // CMP_STRIDE, LANES)
    const3 = lambda i: (0, 0, 0)
    out = jax.ShapeDtypeStruct((b, nch, LANES), F32)
    return pl.pallas_call(
        _compress_kernel,
        grid=(b,),
        in_specs=[pl.BlockSpec((1, l, LANES), lambda i: (i, 0, 0)),
                  pl.BlockSpec((1, l, LANES), lambda i: (i, 0, 1)),
                  pl.BlockSpec((2, CMP_BLOCK, LANES), const3), pl.BlockSpec((2, CMP_BLOCK, LANES), const3),
                  pl.BlockSpec((2, LANES, LANES), const3), pl.BlockSpec((2, LANES, LANES), const3)],
        out_specs=[pl.BlockSpec((1, nch, LANES), lambda i: (i, 0, 0))] * 2,
        out_shape=(out, out),
        compiler_params=_cparams(("parallel",)),
        name="compress",
    )(rows3d, rows3d, mix2, pe2, w1bd, w2bd)


def _masked_softmax0(s, mask):
    s = jnp.where(mask, s, NEG_INF)
    m = jnp.max(s, axis=0, keepdims=True)
    e = jnp.where(mask, jnp.exp(s - m), 0.0)
    return e / jnp.maximum(jnp.sum(e, axis=0, keepdims=True), 1e-30)


def _flash_t(q_aug, k_ref, v_ref, lo, hi, k_aug_fn, mask_fn, kt_rows, zero_masked):
    r = q_aug.shape[1]

    def body(kt, carry):
        m, l, acc = carry
        off = pl.multiple_of(kt * kt_rows, kt_rows)
        k = k_aug_fn(kt, k_ref[0, pl.ds(off, kt_rows), :].astype(BF16))
        vt = v_ref[0, pl.ds(off, kt_rows), :].astype(F32).T.astype(BF16)
        s = jnp.dot(k, q_aug, preferred_element_type=F32)
        mask = mask_fn(kt)
        s = jnp.where(mask, s, NEG_INF)
        m_new = jnp.maximum(m, jnp.max(s, axis=0, keepdims=True))
        p = jnp.exp(s - m_new)
        if zero_masked:
            p = jnp.where(mask, p, 0.0)
        alpha = jnp.exp(m - m_new)
        l = alpha * l + jnp.sum(p, axis=0, keepdims=True)
        acc = alpha * acc + jnp.dot(vt, p.astype(BF16), preferred_element_type=F32)
        return m_new, l, acc

    init = (jnp.full((1, r), NEG_INF, F32), jnp.zeros((1, r), F32), jnp.zeros((LANES, r), F32))
    _, l, acc = lax.fori_loop(lo, hi, body, init)
    return acc / jnp.maximum(l, 1e-30)


def _nsa_kernel(*refs, **static):
    _nsa_body(pl.program_id(1), *refs, **static)


def _nsa_paged_kernel(pt_ref, qt_ref, gt_ref, kc_ref, vc_ref, cover_ref, cache_ref, tail_ref, kw_ref, vw_ref,
                      o_ref, kbuf_ref, vbuf_ref, sem_ref, **static):
    rows = _PagedRows(pt_ref, cache_ref, tail_ref, (kbuf_ref, vbuf_ref), sem_ref, 2)
    slot = rows.fetch()
    _nsa_body(0, qt_ref, gt_ref, kc_ref, vc_ref, cover_ref, rows.half(slot, 0), rows.half(slot, 1), kw_ref, vw_ref,
              o_ref, **static)


def _nsa_body(qi, qt_ref, gt_ref, kc_ref, vc_ref, cover_ref, ks_ref, vs_ref, kw_ref, vw_ref, o_ref,
              *, tq, q_start, kw_start, n_top, kt_sel, kt_win):
    r = NSA_HEADS * tq
    t0 = q_start + qi * tq
    qt = qt_ref[0, 0]
    t_col = t0 + (lax.broadcasted_iota(jnp.int32, (1, r), 1) & (tq - 1))

    kc = kc_ref[0].astype(BF16)
    nch = kc.shape[0]
    s_c = jnp.dot(kc, qt, preferred_element_type=F32)
    cmp_end = lax.broadcasted_iota(jnp.int32, (nch, 1), 0) * CMP_STRIDE + (CMP_BLOCK - 1)
    p_c = _masked_softmax0(s_c, cmp_end <= t_col)
    o_c = jnp.dot(vc_ref[0].T.astype(BF16), p_c.astype(BF16), preferred_element_type=F32)

    ns_pad = cover_ref.shape[0]
    aligned = tq % LANES == 0
    lane_r = lax.broadcasted_iota(jnp.int32, (1, r), 1)
    cur = ((t0 + lax.broadcasted_iota(jnp.int32, (1, tq), 1)) if aligned else t_col) >> 6
    jb = lax.broadcasted_iota(jnp.int32, (ns_pad, 1), 0)
    jbf = jb.astype(F32)
    elig = jb <= cur
    forced = (jb == 0) | (jb == cur) | (jb == cur - 1)
    notsel = []
    for g in range(NSA_KV_HEADS):
        base = g * GROUP * tq
        if aligned:
            pg = p_c[:, base:base + tq]
            for n in range(1, GROUP):
                pg = pg + p_c[:, base + n * tq:base + (n + 1) * tq]
        else:
            pg = p_c
            for n in range(1, GROUP):
                pg = pg + pltpu.roll(p_c, r - n * tq, 1)
        imp = jnp.dot(cover_ref[...], pg, precision=HI, preferred_element_type=F32)
        score = jnp.where(elig, jnp.where(forced, SEL_FORCE, imp), -SEL_FORCE)
        sel = jnp.zeros(score.shape, F32)
        for _ in range(n_top):
            m = jnp.max(score, axis=0, keepdims=True)
            first = jnp.min(jnp.where(score == m, jbf, 1e9), axis=0, keepdims=True)
            hit = jbf == first
            sel = jnp.where(hit, 1.0, sel)
            score = jnp.where(hit, -3e38, score)
        ns = jnp.where(elig, 1.0 - sel, 1.0)
        if not aligned:
            ns = jnp.where((lane_r >= base) & (lane_r < base + tq), ns, 0.0)
            rep = ns
            for n in range(1, GROUP):
                rep = rep + pltpu.roll(ns, n * tq, 1)
            ns = rep
        notsel.append(ns)
    if aligned:
        notsel_cols = jnp.concatenate([notsel[0]] * GROUP + [notsel[1]] * GROUP, axis=1).astype(BF16)
    else:
        notsel_cols = (notsel[0] + notsel[1]).astype(BF16)

    q_aug = jnp.concatenate([qt, notsel_cols], axis=0)

    def sel_keys(kt, k):
        blk = kt * (kt_sel // SEL_BLOCK) + (lax.broadcasted_iota(jnp.int32, (kt_sel, ns_pad), 0) >> 6)
        tag = jnp.where(lax.broadcasted_iota(jnp.int32, (kt_sel, ns_pad), 1) == blk, -SEL_MASK_BIG, 0.0)
        return jnp.concatenate([k, tag.astype(BF16)], axis=1)

    def sel_mask(kt):
        kpos = kt * kt_sel + lax.broadcasted_iota(jnp.int32, (kt_sel, 1), 0)
        return kpos <= t_col

    hi_s = (t0 + tq - 1) // kt_sel + 1
    o_s = _flash_t(q_aug, ks_ref, vs_ref, 0, hi_s, sel_keys, sel_mask, kt_sel, zero_masked=False)

    def win_mask(kt):
        kpos = kw_start + kt * kt_win + lax.broadcasted_iota(jnp.int32, (kt_win, 1), 0)
        dist = t_col - kpos
        return (dist >= 0) & (dist <= WINDOW)

    n_wt = kw_ref.shape[1] // kt_win
    lo_w = jnp.maximum(t0 - WINDOW - kw_start, 0) // kt_win
    hi_w = jnp.minimum((t0 + tq - 1 - kw_start) // kt_win + 1, n_wt)
    o_w = _flash_t(qt, kw_ref, vw_ref, lo_w, hi_w, lambda kt, k: k, win_mask, kt_win, zero_masked=True)

    gt = gt_ref[0, 0]
    out = gt[0:1] * o_c + gt[1:2] * o_s + gt[2:3] * o_w
    o_ref[0, 0] = out.astype(o_ref.dtype)


def _cover_matrix_t(nch, ns_pad):
    ci = np.arange(nch)[None, :] * CMP_STRIDE
    sj = np.arange(ns_pad)[:, None] * SEL_BLOCK
    return jnp.asarray(((ci < sj + SEL_BLOCK) & (ci + CMP_BLOCK > sj)).astype(np.float32))


def _block_diag_qt(q, b, sq, tq):
    nqt = sq // tq
    qr = q.reshape(b, nqt, tq, NSA_KV_HEADS, GROUP, HEAD_DIM).transpose(0, 1, 3, 5, 4, 2)
    z = jnp.zeros_like(qr[:, :, 0])
    top = jnp.stack([qr[:, :, 0], z], axis=3)
    bot = jnp.stack([z, qr[:, :, 1]], axis=3)
    return jnp.stack([top, bot], axis=2).reshape(b, nqt, LANES, NSA_HEADS * tq)


def _nsa_specs(nch, ns_pad, lw, tq, kw_col, idx):
    r = NSA_HEADS * tq
    return dict(
        qt=pl.BlockSpec((1, 1, LANES, r), idx(lambda i, j: (i, j, 0, 0))),
        gt=pl.BlockSpec((1, 1, GATE_ROWS, r), idx(lambda i, j: (i, j, 0, 0))),
        kc=pl.BlockSpec((1, nch, LANES), idx(lambda i, j: (i, 0, 0))),
        cover=pl.BlockSpec((ns_pad, nch), idx(lambda i, j: (0, 0))),
        kw=pl.BlockSpec((1, lw, LANES), idx(lambda i, j: (i, 0, kw_col))),
        vw=pl.BlockSpec((1, lw, LANES), idx(lambda i, j: (i, 0, kw_col + 1))),
        out=pl.BlockSpec((1, 1, LANES, r), idx(lambda i, j: (i, j, 0, 0))))


def _gate_cols(gn, b, sq, tq):
    nqt = sq // tq
    g = gn[:, :3 * NSA_HEADS].reshape(b, nqt, tq, 3, NSA_HEADS).transpose(0, 1, 3, 4, 2)
    return jnp.pad(g.reshape(b, nqt, 3, NSA_HEADS * tq), ((0, 0), (0, 0), (0, GATE_ROWS - 3), (0, 0)))


def _own_head_rows(o, b, sq, tq):
    nqt = sq // tq
    o = o.reshape(b, nqt, NSA_KV_HEADS, HEAD_DIM, NSA_KV_HEADS, GROUP, tq)
    own = jnp.stack([o[:, :, g, :, g] for g in range(NSA_KV_HEADS)], axis=2)
    return own.transpose(0, 1, 5, 2, 4, 3).reshape(b * sq, NSA_WIDTH)


def _key_tile(rows, most):
    for c in range(most, 0, -1):
        if rows % (c * KEY_TILE) == 0:
            return c * KEY_TILE
    raise ValueError("row count must be a multiple of the key tile")


def _nsa(qt, gt, kc, vc, ksv, ks_col, kwv, kw_col, *, tq, q_start, kw_start, ns_true):
    b, nqt = qt.shape[:2]
    nch = kc.shape[1]
    lk = ksv.shape[1]
    ns_pad = _round_up(lk // SEL_BLOCK, LANES)
    sp = _nsa_specs(nch, ns_pad, kwv.shape[1], tq, kw_col, lambda f: f)
    kern = functools.partial(_nsa_kernel, tq=tq, q_start=q_start, kw_start=kw_start, n_top=min(SEL_TOP, ns_true),
                             kt_sel=_key_tile(lk, 4), kt_win=_key_tile(kwv.shape[1], 2))
    return pl.pallas_call(
        kern,
        grid=(b, nqt),
        in_specs=[sp["qt"], sp["gt"], sp["kc"], sp["kc"], sp["cover"],
                  pl.BlockSpec((1, lk, LANES), lambda i, j: (i, 0, ks_col)),
                  pl.BlockSpec((1, lk, LANES), lambda i, j: (i, 0, ks_col + 1)),
                  sp["kw"], sp["vw"]],
        out_specs=sp["out"],
        out_shape=jax.ShapeDtypeStruct((b, nqt, LANES, NSA_HEADS * tq), BF16),
        compiler_params=_cparams(("parallel", "arbitrary")),
        name="nsa",
    )(qt, gt, kc, vc, _cover_matrix_t(nch, ns_pad), ksv, ksv, kwv, kwv)


def _nsa_paged(qt, gt, kc, vc, page_table, cache2d, tail, kwv, *, tq, q_start, kw_start, ns_true):
    db, nqt = qt.shape[:2]
    assert nqt == 1
    nch = kc.shape[1]
    lk = (page_table.shape[1] + 1) * PAGE
    ns_pad = _round_up(lk // SEL_BLOCK, LANES)
    sp = _nsa_specs(nch, ns_pad, kwv.shape[1], tq, 0, lambda f: (lambda i, pt: f(i, 0)))
    kern = functools.partial(_nsa_paged_kernel, tq=tq, q_start=q_start, kw_start=kw_start, n_top=min(SEL_TOP, ns_true),
                             kt_sel=_key_tile(lk, 5), kt_win=_key_tile(kwv.shape[1], 5))
    any_spec = pl.BlockSpec(memory_space=pl.ANY)
    return pl.pallas_call(
        kern,
        grid_spec=pltpu.PrefetchScalarGridSpec(
            num_scalar_prefetch=1, grid=(db,),
            in_specs=[sp["qt"], sp["gt"], sp["kc"], sp["kc"], sp["cover"], any_spec, any_spec, sp["kw"], sp["vw"]],
            out_specs=sp["out"],
            scratch_shapes=[pltpu.VMEM((2, lk, KV_LANES), cache2d.dtype)] * 2 + [pltpu.SemaphoreType.DMA((2,))]),
        out_shape=jax.ShapeDtypeStruct((db, 1, LANES, NSA_HEADS * tq), BF16),
        compiler_params=_cparams(("arbitrary",)),
        name="nsa_paged",
    )(page_table, qt, gt, kc, vc, _cover_matrix_t(nch, ns_pad), cache2d, tail, kwv, kwv)


def _rwkv_pre_kernel(c_ref, p_ref, mu_ref, w0_ref, a0_ref, kkw_ref, ka_ref, wl_ref, g2_ref,
                     r_ref, k_ref, v_ref, kk_ref, a_ref, d_ref, g_ref):
    cols = c_ref[...]
    xs = cols + mu_ref[...] * (p_ref[...] - cols)
    o1, o2, o3 = RW_WIDTH, 2 * RW_WIDTH, 3 * RW_WIDTH
    o5 = o3 + RW_DECAY_LORA + RW_AAA_LORA
    k = xs[:, o1:o2]
    wa = xs[:, o3:o5]
    lane = lax.broadcasted_iota(jnp.int32, wa.shape, 1)
    wa = jnp.where(lane < RW_DECAY_LORA, jnp.tanh(wa), wa)
    lo = jnp.dot(wa.astype(BF16), wl_ref[...], preferred_element_type=F32)
    z = -(w0_ref[...] + lo[:, :RW_WIDTH])
    softplus = jnp.maximum(z, 0.0) + jnp.log(1.0 + jnp.exp(-jnp.abs(z)))
    w = -softplus - 0.5
    a = _sigmoid(a0_ref[...] + lo[:, RW_WIDTH:])
    r_ref[...] = xs[:, :o1]
    k_ref[...] = k * (1.0 + (a - 1.0) * ka_ref[...])
    v_ref[...] = xs[:, o2:o3]
    kk_ref[...] = k * kkw_ref[...]
    a_ref[...] = a
    d_ref[...] = jnp.exp(-jnp.exp(w))
    g_ref[...] = jnp.dot(_sigmoid(xs[:, o5:]).astype(BF16), g2_ref[...], preferred_element_type=F32)


def _rwkv_pre(cols, prev, row0, nrows, mu, w0, a0, kkw, ka, wlora, g2):
    tm = TOK_TILE
    assert row0 % tm == 0 and nrows % tm == 0 and row0 + nrows <= cols.shape[0]
    tile0 = row0 // tm
    row = lambda i: (i, 0)
    src = lambda i: (i + tile0, 0)
    const = lambda i: (0, 0)
    vec = pl.BlockSpec((1, RW_WIDTH), const)
    out = jax.ShapeDtypeStruct((nrows, RW_WIDTH), F32)
    return pl.pallas_call(
        _rwkv_pre_kernel,
        grid=(nrows // tm,),
        in_specs=[pl.BlockSpec((tm, RW_COLS), src), pl.BlockSpec((tm, RW_COLS), src),
                  pl.BlockSpec((1, RW_COLS), const), vec, vec, vec, vec,
                  pl.BlockSpec((RW_DECAY_LORA + RW_AAA_LORA, 2 * RW_WIDTH), const),
                  pl.BlockSpec((RW_GATE_LORA, RW_WIDTH), const)],
        out_specs=[pl.BlockSpec((tm, RW_WIDTH), row)] * 7,
        out_shape=(out,) * 7,
        compiler_params=_cparams(("parallel",)),
        name="rwkv_pre",
    )(cols, prev, mu, w0, a0, kkw, ka, wlora, g2)


def _rwkv_scan_kernel(r_ref, k_ref, v_ref, kk_ref, a_ref, d_ref, g_ref, s0_ref, lnw_ref, lnb_ref, rk_ref,
                      o_ref, sf_ref, st_ref, kkn_ref, b_ref):
    n = RW_HEAD_DIM
    tb = r_ref.shape[0]
    ti = pl.program_id(1)

    @pl.when(ti == 0)
    def _():
        st_ref[...] = s0_ref[...]

    def step(t, c):
        kk = kk_ref[t]
        nrm = jnp.sqrt(jnp.sum(kk * kk, axis=0, keepdims=True))
        kkn = kk / jnp.maximum(nrm, 1e-12)
        kkn_ref[...] = kkn
        b_ref[...] = kkn * a_ref[t]
        vv = v_ref[t]
        sa = jnp.zeros((n, LANES), F32)
        for i in range(n):
            sa = sa - st_ref[i] * kkn_ref[pl.ds(i, 1), :]
        y = jnp.zeros((n, LANES), F32)
        for i in range(n):
            s_new = (st_ref[i] * d_ref[t, pl.ds(i, 1), :] + sa * b_ref[pl.ds(i, 1), :]
                     + vv * k_ref[t, pl.ds(i, 1), :])
            st_ref[i] = s_new
            y = y + s_new * r_ref[t, pl.ds(i, 1), :]
        mu = jnp.mean(y, axis=0, keepdims=True)
        yc = y - mu
        var = jnp.mean(yc * yc, axis=0, keepdims=True)
        yn = yc * lax.rsqrt(var + GN_EPS) * lnw_ref[...] + lnb_ref[...]
        bonus = jnp.sum(r_ref[t] * k_ref[t] * rk_ref[...], axis=0, keepdims=True) * vv
        o_ref[t] = (yn + bonus) * g_ref[t]
        return c

    lax.fori_loop(0, tb, step, 0)

    @pl.when(ti == pl.num_programs(1) - 1)
    def _():
        sf_ref[...] = st_ref[...]


def _rwkv_scan(seqs, s0, lnw, lnb, rk, tb):
    s, n, nl = seqs[0].shape
    blk = pl.BlockSpec((tb, n, LANES), lambda j, i: (i, 0, j))
    st = pl.BlockSpec((n, n, LANES), lambda j, i: (0, 0, j))
    vec = pl.BlockSpec((n, LANES), lambda j, i: (0, j))
    return pl.pallas_call(
        _rwkv_scan_kernel,
        grid=(nl // LANES, s // tb),
        in_specs=[blk] * 7 + [st, vec, vec, vec],
        out_specs=[blk, st],
        out_shape=(jax.ShapeDtypeStruct((s, n, nl), F32), jax.ShapeDtypeStruct((n, n, nl), F32)),
        scratch_shapes=[pltpu.VMEM((n, n, LANES), F32), pltpu.VMEM((n, LANES), F32), pltpu.VMEM((n, LANES), F32)],
        compiler_params=_cparams(("parallel", "arbitrary")),
        name="rwkv_scan",
    )(*seqs, s0, lnw, lnb, rk)


def _to_scan_layout(x, b, s):
    return x.reshape(b, s, RW_HEADS, RW_HEAD_DIM).transpose(1, 3, 0, 2).reshape(s, RW_HEAD_DIM, b * RW_HEADS)


def _from_scan_layout(y, b, s):
    return y.reshape(s, RW_HEAD_DIM, b, RW_HEADS).transpose(2, 0, 3, 1).reshape(b * s, RW_WIDTH)


def _rwkv_group(pre, b, s, wkv0, lnw, lnb, rk):
    nl = b * RW_HEADS
    nlp = _round_up(nl, LANES)

    def padl(x):
        return x if nlp == nl else jnp.pad(x, [(0, 0)] * (x.ndim - 1) + [(0, nlp - nl)])

    seqs = [padl(_to_scan_layout(x, b, s)) for x in pre]
    s0 = padl(wkv0.astype(F32).transpose(3, 2, 0, 1).reshape(RW_HEAD_DIM, RW_HEAD_DIM, nl))
    tile = lambda p: padl(jnp.tile(p.reshape(RW_HEADS, RW_HEAD_DIM).T, (1, b)))
    tb = math.gcd(s, 32)
    o, sf = _rwkv_scan(seqs, s0, tile(lnw), tile(lnb), tile(rk), tb)
    o = _from_scan_layout(o[:, :, :nl], b, s)
    sf = sf[:, :, :nl].reshape(RW_HEAD_DIM, RW_HEAD_DIM, b, RW_HEADS).transpose(2, 3, 1, 0)
    return o, sf


def _merge_kernel(x_ref, on_ref, or_ref, mg_ref, wpa_ref, wpb_ref, wo_ref, n2_ref, wr_ref, br_ref,
                  h_ref, xn_ref, te_ref, tg_ref):
    pa = jnp.dot(on_ref[...], wpa_ref[...], preferred_element_type=F32)
    pb = jnp.dot(or_ref[...].astype(BF16), wpb_ref[...], preferred_element_type=F32)
    merged = mg_ref[:, :D_MODEL] * pa + mg_ref[:, D_MODEL:] * pb
    h = x_ref[...] + jnp.dot(merged.astype(BF16), wo_ref[...], preferred_element_type=F32)
    h_ref[...] = h
    ms = jnp.mean(h * h, axis=-1, keepdims=True)
    xn = h * lax.rsqrt(ms + RMS_EPS) * n2_ref[...]
    xn_ref[...] = xn
    logits = jnp.dot(xn, wr_ref[...], precision=HI, preferred_element_type=F32) + br_ref[...]
    lane = lax.broadcasted_iota(jnp.int32, logits.shape, 1)
    lanef = lane.astype(F32)
    te = jnp.zeros(logits.shape, F32)
    tv = jnp.full(logits.shape, NEG_INF, F32)
    for kx in range(TOP_K):
        m = jnp.max(logits, axis=-1, keepdims=True)
        first = jnp.min(jnp.where(logits == m, lanef, 1e9), axis=-1, keepdims=True)
        te = jnp.where(lane == kx, first, te)
        tv = jnp.where(lane == kx, m, tv)
        logits = jnp.where(lanef == first, -3e38, logits)
    e = jnp.where(lane < TOP_K, jnp.exp(tv - jnp.max(tv, axis=-1, keepdims=True)), 0.0)
    te_ref[...] = te.astype(jnp.int32)
    tg_ref[...] = e / jnp.sum(e, axis=-1, keepdims=True)


def _merge(x, o_nsa, o_rw, mg, wpa, wpb, wo, norm2, wr_pad, br_pad):
    t = x.shape[0]
    tm = TOK_TILE
    row = lambda i: (i, 0)
    const = lambda i: (0, 0)
    outs = (jax.ShapeDtypeStruct((t, D_MODEL), F32), jax.ShapeDtypeStruct((t, D_MODEL), F32),
            jax.ShapeDtypeStruct((t, LANES), jnp.int32), jax.ShapeDtypeStruct((t, LANES), F32))
    return pl.pallas_call(
        _merge_kernel,
        grid=(t // tm,),
        in_specs=[pl.BlockSpec((tm, D_MODEL), row), pl.BlockSpec((tm, NSA_WIDTH), row),
                  pl.BlockSpec((tm, RW_WIDTH), row), pl.BlockSpec((tm, 2 * D_MODEL), row),
                  pl.BlockSpec((NSA_WIDTH, D_MODEL), const), pl.BlockSpec((RW_WIDTH, D_MODEL), const),
                  pl.BlockSpec((D_MODEL, D_MODEL), const), pl.BlockSpec((1, D_MODEL), const),
                  pl.BlockSpec((D_MODEL, LANES), const), pl.BlockSpec((1, LANES), const)],
        out_specs=[pl.BlockSpec((tm, o.shape[1]), row) for o in outs],
        out_shape=outs,
        compiler_params=_cparams(("parallel",)),
        name="merge",
    )(x, o_nsa, o_rw, mg, wpa, wpb, wo, norm2.reshape(1, D_MODEL), wr_pad, br_pad)


class _GatheredRows:
    def __init__(self, src_ref, buf_ref, sem_ref):
        self.src, self.buf, self.sem = src_ref, buf_ref, sem_ref
        self.n = buf_ref.shape[1]

    def start(self, idx_ref, off, slot):
        def body(j, c):
            pltpu.make_async_copy(self.src.at[pl.ds(idx_ref[off + j], 1)], self.buf.at[slot, pl.ds(j, 1)],
                                  self.sem.at[slot]).start()
            return c
        lax.fori_loop(0, self.n, body, 0, unroll=8)

    def start_unrolled(self, idx_ref, off, slot):
        for j in range(self.n):
            pltpu.make_async_copy(self.src.at[pl.ds(idx_ref[off + j], 1)], self.buf.at[slot, pl.ds(j, 1)],
                                  self.sem.at[slot]).start()

    def wait(self, slot):
        pltpu.make_async_copy(self.src.at[pl.ds(0, self.n)], self.buf.at[slot], self.sem.at[slot]).wait()


def _expert_kernel(be_ref, nu_ref, idx_ref, idxn_ref, xn_ref, wu_ref, bu_ref, wd_ref, bd_ref, o_ref, xbuf_ref, sem_ref):
    i = pl.program_id(0)
    n_used = nu_ref[0]
    slot = i & 1
    bm = xbuf_ref.shape[1]
    rows = _GatheredRows(xn_ref, xbuf_ref, sem_ref)

    @pl.when(i == 0)
    def _():
        rows.start(idx_ref, 0, slot)

    @pl.when(i < n_used)
    def _():
        rows.wait(slot)
        x = xbuf_ref[slot].astype(BF16)
        hu = jnp.dot(x, wu_ref[0], preferred_element_type=F32) + bu_ref[0]
        rows.start_unrolled(idxn_ref, (1 - slot) * bm, 1 - slot)
        gt = jnp.minimum(hu[:, :D_FF], SWIGLU_LIMIT)
        up = jnp.clip(hu[:, D_FF:], -SWIGLU_LIMIT, SWIGLU_LIMIT)
        hh = (up + 1.0) * gt * _sigmoid(SWIGLU_ALPHA * gt)
        o_ref[...] = jnp.dot(hh.astype(BF16), wd_ref[0], preferred_element_type=F32) + bd_ref[0]

    @pl.when(i + 1 == n_used)
    def _():
        rows.wait(1 - slot)

    @pl.when(i >= n_used)
    def _():
        o_ref[...] = jnp.zeros(o_ref.shape, o_ref.dtype)


def _experts(blk_exp, n_used, slot_tok, xn, w_up, b_up, w_down, b_down):
    p = slot_tok.shape[0]
    bm = MOE_BM
    nb = p // bm
    assert nb % 2 == 0
    wmap = lambda i, be, nu: (be[i], 0, 0)
    return pl.pallas_call(
        _expert_kernel,
        grid_spec=pltpu.PrefetchScalarGridSpec(
            num_scalar_prefetch=2, grid=(nb,),
            in_specs=[pl.BlockSpec((2 * bm,), lambda i, be, nu: (i // 2,), memory_space=pltpu.SMEM),
                      pl.BlockSpec((2 * bm,), lambda i, be, nu: (jnp.minimum(i + 1, nb - 1) // 2,),
                                   memory_space=pltpu.SMEM),
                      pl.BlockSpec(memory_space=pl.ANY),
                      pl.BlockSpec((1, D_MODEL, 2 * D_FF), wmap), pl.BlockSpec((1, 1, 2 * D_FF), wmap),
                      pl.BlockSpec((1, D_FF, D_MODEL), wmap), pl.BlockSpec((1, 1, D_MODEL), wmap)],
            out_specs=pl.BlockSpec((bm, D_MODEL), lambda i, be, nu: (i, 0)),
            scratch_shapes=[pltpu.VMEM((2, bm, D_MODEL), xn.dtype), pltpu.SemaphoreType.DMA((2,))]),
        out_shape=jax.ShapeDtypeStruct((p, D_MODEL), F32),
        compiler_params=_cparams(("arbitrary",)),
        name="experts",
    )(blk_exp, n_used, slot_tok, slot_tok, xn, w_up, b_up.reshape(N_EXPERTS, 1, 2 * D_FF), w_down,
      b_down.reshape(N_EXPERTS, 1, D_MODEL))


def _final_kernel(idx_ref, idxn_ref, h_ref, yb_ref, tg_ref, nf_ref, o_ref, ybuf_ref, sem_ref):
    i = pl.program_id(0)
    slot = i & 1
    tm = h_ref.shape[0]
    rows = _GatheredRows(yb_ref, ybuf_ref, sem_ref)

    @pl.when(i == 0)
    def _():
        rows.start(idx_ref, 0, slot)

    @pl.when(i + 1 < pl.num_programs(0))
    def _():
        rows.start(idxn_ref, 0, 1 - slot)

    rows.wait(slot)
    tg = tg_ref[...]
    f = tg[:, 0:1] * ybuf_ref[slot, pl.ds(0, tm), :]
    for kx in range(1, TOP_K):
        f = f + tg[:, kx:kx + 1] * ybuf_ref[slot, pl.ds(kx * tm, tm), :]
    y = h_ref[...] + f
    ms = jnp.mean(y * y, axis=-1, keepdims=True)
    o_ref[...] = y * lax.rsqrt(ms + RMS_EPS) * nf_ref[...]


def _final(h, yb, slot_of, tg, norm_f):
    t = h.shape[0]
    tm = TOK_TILE
    nt = t // tm
    idx = slot_of.reshape(nt, tm, TOP_K).transpose(0, 2, 1).reshape(nt * TOP_K * tm)
    row = lambda i: (i, 0)
    return pl.pallas_call(
        _final_kernel,
        grid=(nt,),
        in_specs=[pl.BlockSpec((TOP_K * tm,), lambda i: (i,), memory_space=pltpu.SMEM),
                  pl.BlockSpec((TOP_K * tm,), lambda i: (jnp.minimum(i + 1, nt - 1),), memory_space=pltpu.SMEM),
                  pl.BlockSpec((tm, D_MODEL), row), pl.BlockSpec(memory_space=pl.ANY),
                  pl.BlockSpec((tm, LANES), row), pl.BlockSpec((1, D_MODEL), lambda i: (0, 0))],
        out_specs=pl.BlockSpec((tm, D_MODEL), row),
        out_shape=jax.ShapeDtypeStruct((t, D_MODEL), F32),
        scratch_shapes=[pltpu.VMEM((2, TOP_K * tm, D_MODEL), yb.dtype), pltpu.SemaphoreType.DMA((2,))],
        compiler_params=_cparams(("arbitrary",)),
        name="final",
    )(idx, idx, h, yb, tg, norm_f.reshape(1, D_MODEL))


def _routing(top_e, bm):
    t = top_e.shape[0]
    a = t * TOP_K
    e_flat = top_e.reshape(a)
    order = jnp.argsort(e_flat).astype(jnp.int32)
    e_sorted = e_flat[order]
    counts = jnp.bincount(e_flat, length=N_EXPERTS).astype(jnp.int32)
    starts = jnp.cumsum(counts) - counts
    padded = (counts + bm - 1) // bm * bm
    pends = jnp.cumsum(padded)
    pstarts = pends - padded
    slot_sorted = (pstarts[e_sorted] + jnp.arange(a, dtype=jnp.int32) - starts[e_sorted]).astype(jnp.int32)
    n_blocks = _round_up(-(-a // bm) + N_EXPERTS, IDX_BLOCKS)
    p = n_blocks * bm
    blk_exp = jnp.minimum(jnp.sum(jnp.arange(n_blocks)[:, None] * bm >= pends[None, :], axis=1),
                          N_EXPERTS - 1).astype(jnp.int32)
    slot = jnp.arange(p, dtype=jnp.int32)
    e_slot = jnp.repeat(blk_exp, bm)
    j = slot - pstarts[e_slot]
    src = jnp.clip(starts[e_slot] + j, 0, a - 1)
    slot_tok = jnp.where((j < counts[e_slot]) & (slot < pends[-1]), order[src] // TOP_K, 0).astype(jnp.int32)
    slot_of = slot_sorted[jnp.argsort(order)].reshape(t, TOP_K)
    n_used = (pends[-1:] // bm).astype(jnp.int32)
    return slot_tok, slot_of, blk_exp, n_used


def kernel(x_prompt, x_sample, cache_kv, cache_win, state_wkv, state_shift, page_table, norm1, w_in, cmp_pe, cmp_mix, cmp_w1, cmp_w2, rw_mu, rw_w0, rw_w2, rw_a0, rw_a2, rw_g2, rw_kk, rw_ka, rw_rk, rw_ln_w, rw_ln_b, w_pa, w_pb, w_o, norm2, w_router, b_router, w_up, b_up, w_down, b_down, norm_f):
    assert w_in.shape[0] == 1, "single-layer trunk"
    b, s, _ = x_prompt.shape
    db, ds, _ = x_sample.shape
    n_pages = page_table.shape[1]
    assert cache_kv.shape[2] == PAGE
    past = n_pages * PAGE
    nbuf = cache_win.shape[2]
    tp, ts = b * s, db * ds
    t_real = tp + ts
    t_pad = _round_up(t_real, TOK_ALIGN)

    wi = w_in[0]
    w_pad = jnp.concatenate(
        [wi[:, :OFF_GN], jnp.pad(wi[:, OFF_GN:OFF_RW], ((0, 0), (0, GN_PAD - 3 * NSA_HEADS))), wi[:, OFF_RW:]],
        axis=1).astype(BF16)
    tile_g = lambda p: jnp.tile(p, (1, 1, NSA_KV_HEADS))
    mix2, pe2 = tile_g(cmp_mix[0].astype(F32)), tile_g(cmp_pe[0].astype(F32))
    bd = lambda w: jnp.stack([jnp.kron(jnp.eye(NSA_KV_HEADS, dtype=F32), w[i].astype(F32)) for i in range(2)])
    w1bd, w2bd = bd(cmp_w1[0]), bd(cmp_w2[0])
    zl = jnp.zeros((RW_DECAY_LORA, RW_WIDTH), F32)
    wlora = jnp.concatenate([jnp.concatenate([rw_w2[0], zl], axis=1),
                             jnp.concatenate([zl, rw_a2[0]], axis=1)], axis=0).astype(BF16)
    vec = lambda p: p[0].reshape(1, -1).astype(F32)
    wr_pad = jnp.pad(w_router[0].astype(F32), ((0, 0), (0, LANES - N_EXPERTS)))
    br_pad = jnp.concatenate([b_router[0].astype(F32), jnp.full((LANES - N_EXPERTS,), NEG_INF, F32)]).reshape(1, LANES)

    x_all = jnp.concatenate([x_prompt.reshape(tp, D_MODEL), x_sample.reshape(ts, D_MODEL)], axis=0)
    x_all = jnp.pad(x_all, ((0, t_pad - t_real), (0, 0)))
    q_all, kv_all, kvb_all, gn_all, rw_all, mg_all = _proj(x_all, norm1[0], w_pad)

    kv_p = kv_all[:tp].reshape(b, s, 6 * KV_LANES)
    kv_s = kv_all[tp:t_real].reshape(db, ds, 6 * KV_LANES)
    kvb_p = kvb_all[:tp].reshape(b, s, 6 * KV_LANES)

    tq = NSA_TQ
    assert s % tq == 0 and ds <= PAGE
    kc_p, vc_p = _compress(kv_p, mix2, pe2, w1bd, w2bd)
    o_nsa_p = _nsa(_block_diag_qt(q_all[:tp], b, s, tq), _gate_cols(gn_all[:tp], b, s, tq), kc_p, vc_p,
                   kvb_p, 2, kvb_p, 4, tq=tq, q_start=0, kw_start=0, ns_true=s // SEL_BLOCK)

    row_w = 4 * KV_LANES
    tail = jnp.pad(kv_s[:, :, :row_w], ((0, 0), (0, PAGE - ds), (0, 0)))
    cache2d = cache_kv[0].reshape(-1, PAGE, row_w)
    kc_s, vc_s = _compress_paged(page_table, cache2d, tail, mix2, pe2, w1bd, w2bd)
    kw_full = jnp.concatenate([cache_win[0].reshape(db, nbuf, 2 * KV_LANES), kv_s[:, :, row_w:]], axis=1)
    lw = _round_up(nbuf + ds, KEY_TILE)
    kw_pad = jnp.pad(kw_full, ((0, 0), (0, lw - (nbuf + ds)), (0, 0)))
    lp = _round_up(past + ds, SEL_BLOCK)
    tqs = NSA_TQ_FEW if ds <= NSA_TQ_FEW else NSA_TQ
    assert ds <= tqs
    pad_q = lambda x: jnp.pad(x.reshape(db, ds, -1), ((0, 0), (0, tqs - ds), (0, 0))).reshape(db * tqs, -1)
    o_nsa_s = _nsa_paged(_block_diag_qt(pad_q(q_all[tp:t_real]), db, tqs, tqs),
                         _gate_cols(pad_q(gn_all[tp:t_real]), db, tqs, tqs), kc_s, vc_s, page_table, cache2d, tail,
                         kw_pad, tq=tqs, q_start=past, kw_start=past - nbuf, ns_true=lp // SEL_BLOCK)
    o_nsa = jnp.concatenate([_own_head_rows(o_nsa_p, b, s, tq),
                             _own_head_rows(o_nsa_s, db, tqs, tqs).reshape(db, tqs, NSA_WIDTH)[:, :ds].reshape(ts, NSA_WIDTH),
                             jnp.zeros((t_pad - t_real, NSA_WIDTH), BF16)], axis=0)

    rw_p = rw_all[:tp].reshape(b, s, RW_COLS)
    rw_s = rw_all[tp:t_real].reshape(db, ds, RW_COLS)
    prev_p = jnp.concatenate([jnp.zeros((b, 1, RW_COLS), F32), rw_p[:, :-1]], axis=1)
    prev_s = jnp.concatenate([state_shift[0][:, None].astype(F32), rw_s[:, :-1]], axis=1)
    prev_all = jnp.concatenate([prev_p.reshape(tp, RW_COLS), prev_s.reshape(ts, RW_COLS),
                                jnp.zeros((t_pad - t_real, RW_COLS), F32)], axis=0)
    assert tp % TOK_ALIGN == 0
    pre_args = (vec(rw_mu), vec(rw_w0), vec(rw_a0), vec(rw_kk), vec(rw_ka), wlora, rw_g2[0].astype(BF16))
    pre_p = _rwkv_pre(rw_all, prev_all, 0, tp, *pre_args)
    pre_s = _rwkv_pre(rw_all, prev_all, tp, _round_up(ts, TOK_TILE), *pre_args)
    o_rw_p, wkv_p = _rwkv_group(pre_p, b, s, jnp.zeros((b, RW_HEADS, RW_HEAD_DIM, RW_HEAD_DIM), F32),
                                rw_ln_w[0], rw_ln_b[0], rw_rk[0])
    o_rw_s, wkv_s = _rwkv_group([x[:ts] for x in pre_s], db, ds, state_wkv[0],
                                rw_ln_w[0], rw_ln_b[0], rw_rk[0])
    o_rw = jnp.concatenate([o_rw_p, o_rw_s, jnp.zeros((t_pad - t_real, RW_WIDTH), F32)], axis=0)

    h, xn2, te, tg = _merge(x_all, o_nsa, o_rw, mg_all, w_pa[0].astype(BF16), w_pb[0].astype(BF16),
                            w_o[0].astype(BF16), norm2[0], wr_pad, br_pad)

    slot_tok, slot_of, blk_exp, n_used = _routing(te[:, :TOP_K], MOE_BM)
    yb = _experts(blk_exp, n_used, slot_tok, xn2, w_up[0].astype(BF16), b_up[0].astype(F32), w_down[0].astype(BF16),
                  b_down[0].astype(F32))
    y_all = _final(h, yb, slot_of, tg, norm_f)

    y_prompt = y_all[:tp].reshape(b, s, D_MODEL)
    y_sample = y_all[tp:t_real].reshape(db, ds, D_MODEL)
    kv_shape = (4, NSA_KV_HEADS, HEAD_DIM)
    win_shape = (2, NSA_KV_HEADS, HEAD_DIM)
    kv_prompt = kv_p[:, :, :row_w].reshape((1, b, s) + kv_shape)
    kv_sample = kv_s[:, :, :row_w].reshape((1, db, ds) + kv_shape)
    keep_p = min(WINDOW, s)
    win_prompt = kv_p[:, s - keep_p:, row_w:].reshape((1, b, keep_p) + win_shape)
    win_sample = kw_full[:, nbuf + ds - nbuf:].reshape((1, db, nbuf) + win_shape)
    shift_prompt = rw_p[:, -1][None]
    shift_sample = rw_s[:, -1][None]
    return (y_prompt, y_sample, kv_prompt, kv_sample, win_prompt, win_sample,
            wkv_p[None], wkv_s[None], shift_prompt, shift_sample)
```

```python
import functools
import math

import jax
import jax.numpy as jnp
import numpy as np
from jax import lax
from jax.experimental import pallas as pl
from jax.experimental.pallas import tpu as pltpu

F32 = jnp.float32
BF16 = jnp.bfloat16
HI = lax.Precision.HIGHEST

D_MODEL = 1024
NSA_HEADS = 8
NSA_KV_HEADS = 2
HEAD_DIM = 64
GROUP = NSA_HEADS // NSA_KV_HEADS
NSA_WIDTH = NSA_HEADS * HEAD_DIM
KV_LANES = NSA_KV_HEADS * HEAD_DIM
CMP_BLOCK = 32
CMP_STRIDE = 16
SEL_BLOCK = 64
SEL_TOP = 16
WINDOW = 512
SEL_FORCE = 1e9
NEG_INF = -1e30
SEL_MASK_BIG = 2.0 ** 100
PAGE = 128

RW_HEADS = 8
RW_HEAD_DIM = 64
RW_WIDTH = RW_HEADS * RW_HEAD_DIM
RW_DECAY_LORA = 64
RW_AAA_LORA = 64
RW_GATE_LORA = 128
RW_COLS = 3 * RW_WIDTH + RW_DECAY_LORA + RW_AAA_LORA + RW_GATE_LORA
GN_EPS = 64e-5

N_EXPERTS = 32
TOP_K = 4
D_FF = 1024
SWIGLU_LIMIT = 7.0
SWIGLU_ALPHA = 1.702
RMS_EPS = 1e-5

OFF_KV = NSA_WIDTH
OFF_GN = OFF_KV + 6 * KV_LANES
OFF_RW = OFF_GN + 3 * NSA_HEADS
OFF_MG = OFF_RW + RW_COLS
N_PROJ = OFF_MG + 2 * D_MODEL

LANES = 128
GN_PAD = LANES
P_Q = 0
P_KV = P_Q + NSA_WIDTH
P_GN = P_KV + 6 * KV_LANES
P_RW = P_GN + GN_PAD
P_MG = P_RW + RW_COLS
P_END = P_MG + 2 * D_MODEL

VMEM_LIMIT = 52 * 1024 * 1024

TOK_TILE = 256
TOK_ALIGN = 512
MOE_BM = 512
IDX_BLOCKS = 2
KEY_TILE = 128
NSA_TQ = 256
NSA_TQ_FEW = 16
GATE_ROWS = 8


def _cparams(sem):
    return pltpu.CompilerParams(dimension_semantics=sem, vmem_limit_bytes=VMEM_LIMIT)


def _sigmoid(x):
    return 1.0 / (1.0 + jnp.exp(-x))


def _round_up(a, b):
    return -(-a // b) * b


def _proj_kernel(x_ref, g_ref, w_ref, q_ref, kv_ref, kvb_ref, gn_ref, rw_ref, mg_ref):
    x = x_ref[...]
    ms = jnp.mean(x * x, axis=-1, keepdims=True)
    xn = (x * lax.rsqrt(ms + RMS_EPS) * g_ref[...]).astype(BF16)

    def seg(a, b):
        return jnp.dot(xn, w_ref[:, a:b], preferred_element_type=F32)

    q_ref[...] = (seg(P_Q, P_KV) * (HEAD_DIM ** -0.5)).astype(BF16)
    kv = seg(P_KV, P_GN)
    kv_ref[...] = kv
    kvb_ref[...] = kv.astype(BF16)
    gn_ref[...] = _sigmoid(seg(P_GN, P_RW))
    rw_ref[...] = seg(P_RW, P_MG)
    mg_ref[...] = _sigmoid(seg(P_MG, P_END))


def _proj(x, norm1, w_pad):
    T = x.shape[0]
    tm = TOK_TILE
    row = lambda i: (i, 0)
    const = lambda i: (0, 0)
    outs = (
        jax.ShapeDtypeStruct((T, NSA_WIDTH), BF16),
        jax.ShapeDtypeStruct((T, 6 * KV_LANES), F32),
        jax.ShapeDtypeStruct((T, 6 * KV_LANES), BF16),
        jax.ShapeDtypeStruct((T, GN_PAD), F32),
        jax.ShapeDtypeStruct((T, RW_COLS), F32),
        jax.ShapeDtypeStruct((T, 2 * D_MODEL), F32),
    )
    return pl.pallas_call(
        _proj_kernel,
        grid=(T // tm,),
        in_specs=[pl.BlockSpec((tm, D_MODEL), row), pl.BlockSpec((1, D_MODEL), const),
                  pl.BlockSpec((D_MODEL, P_END), const)],
        out_specs=[pl.BlockSpec((tm, o.shape[1]), row) for o in outs],
        out_shape=outs,
        compiler_params=_cparams(("parallel",)),
        name="proj",
    )(x, norm1.reshape(1, D_MODEL), w_pad)


class _PagedRows:
    def __init__(self, pt_ref, cache_ref, tail_ref, buf_refs, sem_ref, col0):
        self.pt, self.cache, self.tail, self.bufs, self.sem, self.col0 = pt_ref, cache_ref, tail_ref, buf_refs, sem_ref, col0
        self.n_pages = pt_ref.shape[1]

    def _copies(self, src, row0, slot):
        return [pltpu.make_async_copy(src.at[:, pl.ds((self.col0 + i) * KV_LANES, KV_LANES)],
                                      buf.at[slot, pl.ds(row0, PAGE), :], self.sem.at[slot])
                for i, buf in enumerate(self.bufs)]

    def _page(self, b, p, slot):
        return self._copies(self.cache.at[self.pt[b, p]], p * PAGE, slot)

    def _tail(self, b, slot):
        return self._copies(self.tail.at[b], self.n_pages * PAGE, slot)

    def start(self, b, slot):
        def body(p, c):
            for cp in self._page(b, p, slot):
                cp.start()
            return c
        lax.fori_loop(0, self.n_pages, body, 0)
        for cp in self._tail(b, slot):
            cp.start()

    def wait(self, b, slot):
        def body(p, c):
            for cp in self._page(b, p, slot):
                cp.wait()
            return c
        lax.fori_loop(0, self.n_pages, body, 0)
        for cp in self._tail(b, slot):
            cp.wait()

    def fetch(self):
        b = pl.program_id(0)
        slot = b & 1

        @pl.when(b == 0)
        def _():
            self.start(b, slot)

        @pl.when(b + 1 < pl.num_programs(0))
        def _():
            self.start(b + 1, 1 - slot)

        self.wait(b, slot)
        return slot

    def half(self, slot, i):
        return self.bufs[i].at[pl.ds(slot, 1)]


def _gelu_tanh(x):
    return 0.5 * x * (1.0 + jnp.tanh(math.sqrt(2.0 / math.pi) * (x + 0.044715 * (x * x * x))))


def _compress_paged_kernel(pt_ref, cache_ref, tail_ref, mix_ref, pe_ref, w1_ref, w2_ref, kc_ref, vc_ref,
                           kbuf_ref, vbuf_ref, sem_ref):
    rows = _PagedRows(pt_ref, cache_ref, tail_ref, (kbuf_ref, vbuf_ref), sem_ref, 0)
    slot = rows.fetch()
    _compress_kernel(rows.half(slot, 0), rows.half(slot, 1), mix_ref, pe_ref, w1_ref, w2_ref, kc_ref, vc_ref)


def _compress_paged(page_table, cache2d, tail, mix2, pe2, w1bd, w2bd):
    db, n_pages = page_table.shape
    l = (n_pages + 1) * PAGE
    nch = _round_up(l // CMP_STRIDE, LANES)
    const3 = lambda i, pt: (0, 0, 0)
    any_spec = pl.BlockSpec(memory_space=pl.ANY)
    out = jax.ShapeDtypeStruct((db, nch, LANES), F32)
    return pl.pallas_call(
        _compress_paged_kernel,
        grid_spec=pltpu.PrefetchScalarGridSpec(
            num_scalar_prefetch=1, grid=(db,),
            in_specs=[any_spec, any_spec,
                      pl.BlockSpec((2, CMP_BLOCK, LANES), const3), pl.BlockSpec((2, CMP_BLOCK, LANES), const3),
                      pl.BlockSpec((2, LANES, LANES), const3), pl.BlockSpec((2, LANES, LANES), const3)],
            out_specs=[pl.BlockSpec((1, nch, LANES), lambda i, pt: (i, 0, 0))] * 2,
            scratch_shapes=[pltpu.VMEM((2, l, KV_LANES), cache2d.dtype)] * 2 + [pltpu.SemaphoreType.DMA((2,))]),
        out_shape=(out, out),
        compiler_params=_cparams(("arbitrary",)),
        name="compress_paged",
    )(page_table, cache2d, tail, mix2, pe2, w1bd, w2bd)


def _compress_kernel(rk_ref, rv_ref, mix_ref, pe_ref, w1_ref, w2_ref, kc_ref, vc_ref):
    nch = rk_ref.shape[1] // CMP_STRIDE
    nch_pad = kc_ref.shape[1]
    for idx, (r_ref, o_ref) in enumerate(((rk_ref, kc_ref), (rv_ref, vc_ref))):
        mix = mix_ref[idx]
        cs0 = jnp.zeros((nch, LANES), F32)
        cs1 = jnp.zeros((nch, LANES), F32)
        for p in range(CMP_STRIDE):
            xp = r_ref[0, pl.ds(p, nch, stride=CMP_STRIDE), :].astype(F32)
            cs0 = cs0 + xp * mix[p:p + 1]
            cs1 = cs1 + xp * mix[CMP_STRIDE + p:CMP_STRIDE + p + 1]
        c0 = jnp.sum(mix * pe_ref[idx], axis=0, keepdims=True)
        pre = c0 + cs0 + pltpu.roll(cs1, nch - 1, 0)
        h = _gelu_tanh(jnp.dot(pre, w1_ref[idx], precision=HI, preferred_element_type=F32))
        o_ref[0, pl.ds(0, nch), :] = jnp.dot(h, w2_ref[idx], precision=HI, preferred_element_type=F32)
        if nch_pad > nch:
            o_ref[0, pl.ds(nch, nch_pad - nch), :] = jnp.zeros((nch_pad - nch, LANES), F32)


def _compress(rows3d, mix2, pe2, w1bd, w2bd):
    b, l, _ = rows3d.shape
    nch = _round_up(l // CMP_STRIDE, LANES)
    const3 = lambda i: (0, 0, 0)
    out = jax.ShapeDtypeStruct((b, nch, LANES), F32)
    return pl.pallas_call(
        _compress_kernel,
        grid=(b,),
        in_specs=[pl.BlockSpec((1, l, LANES), lambda i: (i, 0, 0)),
                  pl.BlockSpec((1, l, LANES), lambda i: (i, 0, 1)),
                  pl.BlockSpec((2, CMP_BLOCK, LANES), const3), pl.BlockSpec((2, CMP_BLOCK, LANES), const3),
                  pl.BlockSpec((2, LANES, LANES), const3), pl.BlockSpec((2, LANES, LANES), const3)],
        out_specs=[pl.BlockSpec((1, nch, LANES), lambda i: (i, 0, 0))] * 2,
        out_shape=(out, out),
        compiler_params=_cparams(("parallel",)),
        name="compress",
    )(rows3d, rows3d, mix2, pe2, w1bd, w2bd)


def _masked_softmax0(s, mask):
    s = jnp.where(mask, s, NEG_INF)
    m = jnp.max(s, axis=0, keepdims=True)
    e = jnp.where(mask, jnp.exp(s - m), 0.0)
    return e / jnp.maximum(jnp.sum(e, axis=0, keepdims=True), 1e-30)


def _flash_t(q_aug, k_ref, v_ref, lo, hi, k_aug_fn, mask_fn, kt_rows, zero_masked):
    r = q_aug.shape[1]

    def body(kt, carry):
        m, l, acc = carry
        off = pl.multiple_of(kt * kt_rows, kt_rows)
        k = k_aug_fn(kt, k_ref[0, pl.ds(off, kt_rows), :].astype(BF16))
        vt = v_ref[0, pl.ds(off, kt_rows), :].astype(F32).T.astype(BF16)
        s = jnp.dot(k, q_aug, preferred_element_type=F32)
        mask = mask_fn(kt)
        s = jnp.where(mask, s, NEG_INF)
        m_new = jnp.maximum(m, jnp.max(s, axis=0, keepdims=True))
        p = jnp.exp(s - m_new)
        if zero_masked:
            p = jnp.where(mask, p, 0.0)
        alpha = jnp.exp(m - m_new)
        l = alpha * l + jnp.sum(p, axis=0, keepdims=True)
        acc = alpha * acc + jnp.dot(vt, p.astype(BF16), preferred_element_type=F32)
        return m_new, l, acc

    init = (jnp.full((1, r), NEG_INF, F32), jnp.zeros((1, r), F32), jnp.zeros((LANES, r), F32))
    _, l, acc = lax.fori_loop(lo, hi, body, init)
    return acc / jnp.maximum(l, 1e-30)


def _nsa_kernel(*refs, **static):
    _nsa_body(pl.program_id(1), *refs, **static)


def _nsa_paged_kernel(pt_ref, qt_ref, gt_ref, kc_ref, vc_ref, cover_ref, cache_ref, tail_ref, kw_ref, vw_ref,
                      o_ref, kbuf_ref, vbuf_ref, sem_ref, **static):
    rows = _PagedRows(pt_ref, cache_ref, tail_ref, (kbuf_ref, vbuf_ref), sem_ref, 2)
    slot = rows.fetch()
    _nsa_body(0, qt_ref, gt_ref, kc_ref, vc_ref, cover_ref, rows.half(slot, 0), rows.half(slot, 1), kw_ref, vw_ref,
              o_ref, **static)


def _nsa_body(qi, qt_ref, gt_ref, kc_ref, vc_ref, cover_ref, ks_ref, vs_ref, kw_ref, vw_ref, o_ref,
              *, tq, q_start, kw_start, n_top, kt_sel, kt_win):
    r = NSA_HEADS * tq
    t0 = q_start + qi * tq
    qt = qt_ref[0, 0]
    t_col = t0 + (lax.broadcasted_iota(jnp.int32, (1, r), 1) & (tq - 1))

    kc = kc_ref[0].astype(BF16)
    nch = kc.shape[0]
    s_c = jnp.dot(kc, qt, preferred_element_type=F32)
    cmp_end = lax.broadcasted_iota(jnp.int32, (nch, 1), 0) * CMP_STRIDE + (CMP_BLOCK - 1)
    p_c = _masked_softmax0(s_c, cmp_end <= t_col)
    o_c = jnp.dot(vc_ref[0].T.astype(BF16), p_c.astype(BF16), preferred_element_type=F32)

    ns_pad = cover_ref.shape[0]
    aligned = tq % LANES == 0
    lane_r = lax.broadcasted_iota(jnp.int32, (1, r), 1)
    cur = ((t0 + lax.broadcasted_iota(jnp.int32, (1, tq), 1)) if aligned else t_col) >> 6
    jb = lax.broadcasted_iota(jnp.int32, (ns_pad, 1), 0)
    jbf = jb.astype(F32)
    elig = jb <= cur
    forced = (jb == 0) | (jb == cur) | (jb == cur - 1)
    notsel = []
    for g in range(NSA_KV_HEADS):
        base = g * GROUP * tq
        if aligned:
            pg = p_c[:, base:base + tq]
            for n in range(1, GROUP):
                pg = pg + p_c[:, base + n * tq:base + (n + 1) * tq]
        else:
            pg = p_c
            for n in range(1, GROUP):
                pg = pg + pltpu.roll(p_c, r - n * tq, 1)
        imp = jnp.dot(cover_ref[...], pg, precision=HI, preferred_element_type=F32)
        score = jnp.where(elig, jnp.where(forced, SEL_FORCE, imp), -SEL_FORCE)
        sel = jnp.zeros(score.shape, F32)
        for _ in range(n_top):
            m = jnp.max(score, axis=0, keepdims=True)
            first = jnp.min(jnp.where(score == m, jbf, 1e9), axis=0, keepdims=True)
            hit = jbf == first
            sel = jnp.where(hit, 1.0, sel)
            score = jnp.where(hit, -3e38, score)
        ns = jnp.where(elig, 1.0 - sel, 1.0)
        if not aligned:
            ns = jnp.where((lane_r >= base) & (lane_r < base + tq), ns, 0.0)
            rep = ns
            for n in range(1, GROUP):
                rep = rep + pltpu.roll(ns, n * tq, 1)
            ns = rep
        notsel.append(ns)
    if aligned:
        notsel_cols = jnp.concatenate([notsel[0]] * GROUP + [notsel[1]] * GROUP, axis=1).astype(BF16)
    else:
        notsel_cols = (notsel[0] + notsel[1]).astype(BF16)

    q_aug = jnp.concatenate([qt, notsel_cols], axis=0)

    def sel_keys(kt, k):
        blk = kt * (kt_sel // SEL_BLOCK) + (lax.broadcasted_iota(jnp.int32, (kt_sel, ns_pad), 0) >> 6)
        tag = jnp.where(lax.broadcasted_iota(jnp.int32, (kt_sel, ns_pad), 1) == blk, -SEL_MASK_BIG, 0.0)
        return jnp.concatenate([k, tag.astype(BF16)], axis=1)

    def sel_mask(kt):
        kpos = kt * kt_sel + lax.broadcasted_iota(jnp.int32, (kt_sel, 1), 0)
        return kpos <= t_col

    hi_s = (t0 + tq - 1) // kt_sel + 1
    o_s = _flash_t(q_aug, ks_ref, vs_ref, 0, hi_s, sel_keys, sel_mask, kt_sel, zero_masked=False)

    def win_mask(kt):
        kpos = kw_start + kt * kt_win + lax.broadcasted_iota(jnp.int32, (kt_win, 1), 0)
        dist = t_col - kpos
        return (dist >= 0) & (dist <= WINDOW)

    n_wt = kw_ref.shape[1] // kt_win
    lo_w = jnp.maximum(t0 - WINDOW - kw_start, 0) // kt_win
    hi_w = jnp.minimum((t0 + tq - 1 - kw_start) // kt_win + 1, n_wt)
    o_w = _flash_t(qt, kw_ref, vw_ref, lo_w, hi_w, lambda kt, k: k, win_mask, kt_win, zero_masked=True)

    gt = gt_ref[0, 0]
    out = gt[0:1] * o_c + gt[1:2] * o_s + gt[2:3] * o_w
    o_ref[0, 0] = out.astype(o_ref.dtype)


def _cover_matrix_t(nch, ns_pad):
    ci = np.arange(nch)[None, :] * CMP_STRIDE
    sj = np.arange(ns_pad)[:, None] * SEL_BLOCK
    return jnp.asarray(((ci < sj + SEL_BLOCK) & (ci + CMP_BLOCK > sj)).astype(np.float32))


def _block_diag_qt(q, b, sq, tq):
    nqt = sq // tq
    qr = q.reshape(b, nqt, tq, NSA_KV_HEADS, GROUP, HEAD_DIM).transpose(0, 1, 3, 5, 4, 2)
    z = jnp.zeros_like(qr[:, :, 0])
    top = jnp.stack([qr[:, :, 0], z], axis=3)
    bot = jnp.stack([z, qr[:, :, 1]], axis=3)
    return jnp.stack([top, bot], axis=2).reshape(b, nqt, LANES, NSA_HEADS * tq)


def _nsa_specs(nch, ns_pad, lw, tq, kw_col, idx):
    r = NSA_HEADS * tq
    return dict(
        qt=pl.BlockSpec((1, 1, LANES, r), idx(lambda i, j: (i, j, 0, 0))),
        gt=pl.BlockSpec((1, 1, GATE_ROWS, r), idx(lambda i, j: (i, j, 0, 0))),
        kc=pl.BlockSpec((1, nch, LANES), idx(lambda i, j: (i, 0, 0))),
        cover=pl.BlockSpec((ns_pad, nch), idx(lambda i, j: (0, 0))),
        kw=pl.BlockSpec((1, lw, LANES), idx(lambda i, j: (i, 0, kw_col))),
        vw=pl.BlockSpec((1, lw, LANES), idx(lambda i, j: (i, 0, kw_col + 1))),
        out=pl.BlockSpec((1, 1, LANES, r), idx(lambda i, j: (i, j, 0, 0))))


def _gate_cols(gn, b, sq, tq):
    nqt = sq // tq
    g = gn[:, :3 * NSA_HEADS].reshape(b, nqt, tq, 3, NSA_HEADS).transpose(0, 1, 3, 4, 2)
    return jnp.pad(g.reshape(b, nqt, 3, NSA_HEADS * tq), ((0, 0), (0, 0), (0, GATE_ROWS - 3), (0, 0)))


def _own_head_rows(o, b, sq, tq):
    nqt = sq // tq
    o = o.reshape(b, nqt, NSA_KV_HEADS, HEAD_DIM, NSA_KV_HEADS, GROUP, tq)
    own = jnp.stack([o[:, :, g, :, g] for g in range(NSA_KV_HEADS)], axis=2)
    return own.transpose(0, 1, 5, 2, 4, 3).reshape(b * sq, NSA_WIDTH)


def _key_tile(rows, most):
    for c in range(most, 0, -1):
        if rows % (c * KEY_TILE) == 0:
            return c * KEY_TILE
    raise ValueError("row count must be a multiple of the key tile")


def _nsa(qt, gt, kc, vc, ksv, ks_col, kwv, kw_col, *, tq, q_start, kw_start, ns_true):
    b, nqt = qt.shape[:2]
    nch = kc.shape[1]
    lk = ksv.shape[1]
    ns_pad = _round_up(lk // SEL_BLOCK, LANES)
    sp = _nsa_specs(nch, ns_pad, kwv.shape[1], tq, kw_col, lambda f: f)
    kern = functools.partial(_nsa_kernel, tq=tq, q_start=q_start, kw_start=kw_start, n_top=min(SEL_TOP, ns_true),
                             kt_sel=_key_tile(lk, 4), kt_win=_key_tile(kwv.shape[1], 2))
    return pl.pallas_call(
        kern,
        grid=(b, nqt),
        in_specs=[sp["qt"], sp["gt"], sp["kc"], sp["kc"], sp["cover"],
                  pl.BlockSpec((1, lk, LANES), lambda i, j: (i, 0, ks_col)),
                  pl.BlockSpec((1, lk, LANES), lambda i, j: (i, 0, ks_col + 1)),
                  sp["kw"], sp["vw"]],
        out_specs=sp["out"],
        out_shape=jax.ShapeDtypeStruct((b, nqt, LANES, NSA_HEADS * tq), BF16),
        compiler_params=_cparams(("parallel", "arbitrary")),
        name="nsa",
    )(qt, gt, kc, vc, _cover_matrix_t(nch, ns_pad), ksv, ksv, kwv, kwv)


def _nsa_paged(qt, gt, kc, vc, page_table, cache2d, tail, kwv, *, tq, q_start, kw_start, ns_true):
    db, nqt = qt.shape[:2]
    assert nqt == 1
    nch = kc.shape[1]
    lk = (page_table.shape[1] + 1) * PAGE
    ns_pad = _round_up(lk // SEL_BLOCK, LANES)
    sp = _nsa_specs(nch, ns_pad, kwv.shape[1], tq, 0, lambda f: (lambda i, pt: f(i, 0)))
    kern = functools.partial(_nsa_paged_kernel, tq=tq, q_start=q_start, kw_start=kw_start, n_top=min(SEL_TOP, ns_true),
                             kt_sel=_key_tile(lk, 5), kt_win=_key_tile(kwv.shape[1], 5))
    any_spec = pl.BlockSpec(memory_space=pl.ANY)
    return pl.pallas_call(
        kern,
        grid_spec=pltpu.PrefetchScalarGridSpec(
            num_scalar_prefetch=1, grid=(db,),
            in_specs=[sp["qt"], sp["gt"], sp["kc"], sp["kc"], sp["cover"], any_spec, any_spec, sp["kw"], sp["vw"]],
            out_specs=sp["out"],
            scratch_shapes=[pltpu.VMEM((2, lk, KV_LANES), cache2d.dtype)] * 2 + [pltpu.SemaphoreType.DMA((2,))]),
        out_shape=jax.ShapeDtypeStruct((db, 1, LANES, NSA_HEADS * tq), BF16),
        compiler_params=_cparams(("arbitrary",)),
        name="nsa_paged",
    )(page_table, qt, gt, kc, vc, _cover_matrix_t(nch, ns_pad), cache2d, tail, kwv, kwv)


def _rwkv_pre_kernel(c_ref, p_ref, mu_ref, w0_ref, a0_ref, kkw_ref, ka_ref, wl_ref, g2_ref,
                     r_ref, k_ref, v_ref, kk_ref, a_ref, d_ref, g_ref):
    cols = c_ref[...]
    xs = cols + mu_ref[...] * (p_ref[...] - cols)
    o1, o2, o3 = RW_WIDTH, 2 * RW_WIDTH, 3 * RW_WIDTH
    o5 = o3 + RW_DECAY_LORA + RW_AAA_LORA
    k = xs[:, o1:o2]
    wa = xs[:, o3:o5]
    lane = lax.broadcasted_iota(jnp.int32, wa.shape, 1)
    wa = jnp.where(lane < RW_DECAY_LORA, jnp.tanh(wa), wa)
    lo = jnp.dot(wa.astype(BF16), wl_ref[...], preferred_element_type=F32)
    z = -(w0_ref[...] + lo[:, :RW_WIDTH])
    softplus = jnp.maximum(z, 0.0) + jnp.log(1.0 + jnp.exp(-jnp.abs(z)))
    w = -softplus - 0.5
    a = _sigmoid(a0_ref[...] + lo[:, RW_WIDTH:])
    r_ref[...] = xs[:, :o1]
    k_ref[...] = k * (1.0 + (a - 1.0) * ka_ref[...])
    v_ref[...] = xs[:, o2:o3]
    kk_ref[...] = k * kkw_ref[...]
    a_ref[...] = a
    d_ref[...] = jnp.exp(-jnp.exp(w))
    g_ref[...] = jnp.dot(_sigmoid(xs[:, o5:]).astype(BF16), g2_ref[...], preferred_element_type=F32)


def _rwkv_pre(cols, prev, row0, nrows, mu, w0, a0, kkw, ka, wlora, g2):
    tm = TOK_TILE
    assert row0 % tm == 0 and nrows % tm == 0 and row0 + nrows <= cols.shape[0]
    tile0 = row0 // tm
    row = lambda i: (i, 0)
    src = lambda i: (i + tile0, 0)
    const = lambda i: (0, 0)
    vec = pl.BlockSpec((1, RW_WIDTH), const)
    out = jax.ShapeDtypeStruct((nrows, RW_WIDTH), F32)
    return pl.pallas_call(
        _rwkv_pre_kernel,
        grid=(nrows // tm,),
        in_specs=[pl.BlockSpec((tm, RW_COLS), src), pl.BlockSpec((tm, RW_COLS), src),
                  pl.BlockSpec((1, RW_COLS), const), vec, vec, vec, vec,
                  pl.BlockSpec((RW_DECAY_LORA + RW_AAA_LORA, 2 * RW_WIDTH), const),
                  pl.BlockSpec((RW_GATE_LORA, RW_WIDTH), const)],
        out_specs=[pl.BlockSpec((tm, RW_WIDTH), row)] * 7,
        out_shape=(out,) * 7,
        compiler_params=_cparams(("parallel",)),
        name="rwkv_pre",
    )(cols, prev, mu, w0, a0, kkw, ka, wlora, g2)


def _rwkv_scan_kernel(r_ref, k_ref, v_ref, kk_ref, a_ref, d_ref, g_ref, s0_ref, lnw_ref, lnb_ref, rk_ref,
                      o_ref, sf_ref, st_ref, kkn_ref, b_ref):
    n = RW_HEAD_DIM
    tb = r_ref.shape[0]
    ti = pl.program_id(1)

    @pl.when(ti == 0)
    def _():
        st_ref[...] = s0_ref[...]

    def step(t, c):
        kk = kk_ref[t]
        nrm = jnp.sqrt(jnp.sum(kk * kk, axis=0, keepdims=True))
        kkn = kk / jnp.maximum(nrm, 1e-12)
        kkn_ref[...] = kkn
        b_ref[...] = kkn * a_ref[t]
        vv = v_ref[t]
        sa = jnp.zeros((n, LANES), F32)
        for i in range(n):
            sa = sa - st_ref[i] * kkn_ref[pl.ds(i, 1), :]
        y = jnp.zeros((n, LANES), F32)
        for i in range(n):
            s_new = (st_ref[i] * d_ref[t, pl.ds(i, 1), :] + sa * b_ref[pl.ds(i, 1), :]
                     + vv * k_ref[t, pl.ds(i, 1), :])
            st_ref[i] = s_new
            y = y + s_new * r_ref[t, pl.ds(i, 1), :]
        mu = jnp.mean(y, axis=0, keepdims=True)
        yc = y - mu
        var = jnp.mean(yc * yc, axis=0, keepdims=True)
        yn = yc * lax.rsqrt(var + GN_EPS) * lnw_ref[...] + lnb_ref[...]
        bonus = jnp.sum(r_ref[t] * k_ref[t] * rk_ref[...], axis=0, keepdims=True) * vv
        o_ref[t] = (yn + bonus) * g_ref[t]
        return c

    lax.fori_loop(0, tb, step, 0)

    @pl.when(ti == pl.num_programs(1) - 1)
    def _():
        sf_ref[...] = st_ref[...]


def _rwkv_scan(seqs, s0, lnw, lnb, rk, tb):
    s, n, nl = seqs[0].shape
    blk = pl.BlockSpec((tb, n, LANES), lambda j, i: (i, 0, j))
    st = pl.BlockSpec((n, n, LANES), lambda j, i: (0, 0, j))
    vec = pl.BlockSpec((n, LANES), lambda j, i: (0, j))
    return pl.pallas_call(
        _rwkv_scan_kernel,
        grid=(nl // LANES, s // tb),
        in_specs=[blk] * 7 + [st, vec, vec, vec],
        out_specs=[blk, st],
        out_shape=(jax.ShapeDtypeStruct((s, n, nl), F32), jax.ShapeDtypeStruct((n, n, nl), F32)),
        scratch_shapes=[pltpu.VMEM((n, n, LANES), F32), pltpu.VMEM((n, LANES), F32), pltpu.VMEM((n, LANES), F32)],
        compiler_params=_cparams(("parallel", "arbitrary")),
        name="rwkv_scan",
    )(*seqs, s0, lnw, lnb, rk)


def _to_scan_layout(x, b, s):
    return x.reshape(b, s, RW_HEADS, RW_HEAD_DIM).transpose(1, 3, 0, 2).reshape(s, RW_HEAD_DIM, b * RW_HEADS)


def _from_scan_layout(y, b, s):
    return y.reshape(s, RW_HEAD_DIM, b, RW_HEADS).transpose(2, 0, 3, 1).reshape(b * s, RW_WIDTH)


def _rwkv_group(pre, b, s, wkv0, lnw, lnb, rk):
    nl = b * RW_HEADS
    nlp = _round_up(nl, LANES)

    def padl(x):
        return x if nlp == nl else jnp.pad(x, [(0, 0)] * (x.ndim - 1) + [(0, nlp - nl)])

    seqs = [padl(_to_scan_layout(x, b, s)) for x in pre]
    s0 = padl(wkv0.astype(F32).transpose(3, 2, 0, 1).reshape(RW_HEAD_DIM, RW_HEAD_DIM, nl))
    tile = lambda p: padl(jnp.tile(p.reshape(RW_HEADS, RW_HEAD_DIM).T, (1, b)))
    tb = math.gcd(s, 32)
    o, sf = _rwkv_scan(seqs, s0, tile(lnw), tile(lnb), tile(rk), tb)
    o = _from_scan_layout(o[:, :, :nl], b, s)
    sf = sf[:, :, :nl].reshape(RW_HEAD_DIM, RW_HEAD_DIM, b, RW_HEADS).transpose(2, 3, 1, 0)
    return o, sf


def _merge_kernel(x_ref, on_ref, or_ref, mg_ref, wpa_ref, wpb_ref, wo_ref, n2_ref, wr_ref, br_ref,
                  h_ref, xn_ref, te_ref, tg_ref):
    pa = jnp.dot(on_ref[...], wpa_ref[...], preferred_element_type=F32)
    pb = jnp.dot(or_ref[...].astype(BF16), wpb_ref[...], preferred_element_type=F32)
    merged = mg_ref[:, :D_MODEL] * pa + mg_ref[:, D_MODEL:] * pb
    h = x_ref[...] + jnp.dot(merged.astype(BF16), wo_ref[...], preferred_element_type=F32)
    h_ref[...] = h
    ms = jnp.mean(h * h, axis=-1, keepdims=True)
    xn = h * lax.rsqrt(ms + RMS_EPS) * n2_ref[...]
    xn_ref[...] = xn
    logits = jnp.dot(xn, wr_ref[...], precision=HI, preferred_element_type=F32) + br_ref[...]
    lane = lax.broadcasted_iota(jnp.int32, logits.shape, 1)
    lanef = lane.astype(F32)
    te = jnp.zeros(logits.shape, F32)
    tv = jnp.full(logits.shape, NEG_INF, F32)
    for kx in range(TOP_K):
        m = jnp.max(logits, axis=-1, keepdims=True)
        first = jnp.min(jnp.where(logits == m, lanef, 1e9), axis=-1, keepdims=True)
        te = jnp.where(lane == kx, first, te)
        tv = jnp.where(lane == kx, m, tv)
        logits = jnp.where(lanef == first, -3e38, logits)
    e = jnp.where(lane < TOP_K, jnp.exp(tv - jnp.max(tv, axis=-1, keepdims=True)), 0.0)
    te_ref[...] = te.astype(jnp.int32)
    tg_ref[...] = e / jnp.sum(e, axis=-1, keepdims=True)


def _merge(x, o_nsa, o_rw, mg, wpa, wpb, wo, norm2, wr_pad, br_pad):
    t = x.shape[0]
    tm = TOK_TILE
    row = lambda i: (i, 0)
    const = lambda i: (0, 0)
    outs = (jax.ShapeDtypeStruct((t, D_MODEL), F32), jax.ShapeDtypeStruct((t, D_MODEL), F32),
            jax.ShapeDtypeStruct((t, LANES), jnp.int32), jax.ShapeDtypeStruct((t, LANES), F32))
    return pl.pallas_call(
        _merge_kernel,
        grid=(t // tm,),
        in_specs=[pl.BlockSpec((tm, D_MODEL), row), pl.BlockSpec((tm, NSA_WIDTH), row),
                  pl.BlockSpec((tm, RW_WIDTH), row), pl.BlockSpec((tm, 2 * D_MODEL), row),
                  pl.BlockSpec((NSA_WIDTH, D_MODEL), const), pl.BlockSpec((RW_WIDTH, D_MODEL), const),
                  pl.BlockSpec((D_MODEL, D_MODEL), const), pl.BlockSpec((1, D_MODEL), const),
                  pl.BlockSpec((D_MODEL, LANES), const), pl.BlockSpec((1, LANES), const)],
        out_specs=[pl.BlockSpec((tm, o.shape[1]), row) for o in outs],
        out_shape=outs,
        compiler_params=_cparams(("parallel",)),
        name="merge",
    )(x, o_nsa, o_rw, mg, wpa, wpb, wo, norm2.reshape(1, D_MODEL), wr_pad, br_pad)


class _GatheredRows:
    def __init__(self, src_ref, buf_ref, sem_ref):
        self.src, self.buf, self.sem = src_ref, buf_ref, sem_ref
        self.n = buf_ref.shape[1]

    def start(self, idx_ref, off, slot):
        def body(j, c):
            pltpu.make_async_copy(self.src.at[pl.ds(idx_ref[off + j], 1)], self.buf.at[slot, pl.ds(j, 1)],
                                  self.sem.at[slot]).start()
            return c
        lax.fori_loop(0, self.n, body, 0, unroll=8)

    def start_unrolled(self, idx_ref, off, slot):
        for j in range(self.n):
            pltpu.make_async_copy(self.src.at[pl.ds(idx_ref[off + j], 1)], self.buf.at[slot, pl.ds(j, 1)],
                                  self.sem.at[slot]).start()

    def wait(self, slot):
        pltpu.make_async_copy(self.src.at[pl.ds(0, self.n)], self.buf.at[slot], self.sem.at[slot]).wait()


def _expert_kernel(be_ref, nu_ref, idx_ref, idxn_ref, xn_ref, wu_ref, bu_ref, wd_ref, bd_ref, o_ref, xbuf_ref, sem_ref):
    i = pl.program_id(0)
    n_used = nu_ref[0]
    slot = i & 1
    bm = xbuf_ref.shape[1]
    rows = _GatheredRows(xn_ref, xbuf_ref, sem_ref)

    @pl.when(i == 0)
    def _():
        rows.start(idx_ref, 0, slot)

    @pl.when(i < n_used)
    def _():
        rows.wait(slot)
        x = xbuf_ref[slot].astype(BF16)
        hu = jnp.dot(x, wu_ref[0], preferred_element_type=F32) + bu_ref[0]
        rows.start_unrolled(idxn_ref, (1 - slot) * bm, 1 - slot)
        gt = jnp.minimum(hu[:, :D_FF], SWIGLU_LIMIT)
        up = jnp.clip(hu[:, D_FF:], -SWIGLU_LIMIT, SWIGLU_LIMIT)
        hh = (up + 1.0) * gt * _sigmoid(SWIGLU_ALPHA * gt)
        o_ref[...] = jnp.dot(hh.astype(BF16), wd_ref[0], preferred_element_type=F32) + bd_ref[0]

    @pl.when(i + 1 == n_used)
    def _():
        rows.wait(1 - slot)

    @pl.when(i >= n_used)
    def _():
        o_ref[...] = jnp.zeros(o_ref.shape, o_ref.dtype)


def _experts(blk_exp, n_used, slot_tok, xn, w_up, b_up, w_down, b_down):
    p = slot_tok.shape[0]
    bm = MOE_BM
    nb = p // bm
    assert nb % 2 == 0
    wmap = lambda i, be, nu: (be[i], 0, 0)
    return pl.pallas_call(
        _expert_kernel,
        grid_spec=pltpu.PrefetchScalarGridSpec(
            num_scalar_prefetch=2, grid=(nb,),
            in_specs=[pl.BlockSpec((2 * bm,), lambda i, be, nu: (i // 2,), memory_space=pltpu.SMEM),
                      pl.BlockSpec((2 * bm,), lambda i, be, nu: (jnp.minimum(i + 1, nb - 1) // 2,),
                                   memory_space=pltpu.SMEM),
                      pl.BlockSpec(memory_space=pl.ANY),
                      pl.BlockSpec((1, D_MODEL, 2 * D_FF), wmap), pl.BlockSpec((1, 1, 2 * D_FF), wmap),
                      pl.BlockSpec((1, D_FF, D_MODEL), wmap), pl.BlockSpec((1, 1, D_MODEL), wmap)],
            out_specs=pl.BlockSpec((bm, D_MODEL), lambda i, be, nu: (i, 0)),
            scratch_shapes=[pltpu.VMEM((2, bm, D_MODEL), xn.dtype), pltpu.SemaphoreType.DMA((2,))]),
        out_shape=jax.ShapeDtypeStruct((p, D_MODEL), F32),
        compiler_params=_cparams(("arbitrary",)),
        name="experts",
    )(blk_exp, n_used, slot_tok, slot_tok, xn, w_up, b_up.reshape(N_EXPERTS, 1, 2 * D_FF), w_down,
      b_down.reshape(N_EXPERTS, 1, D_MODEL))


def _final_kernel(idx_ref, idxn_ref, h_ref, yb_ref, tg_ref, nf_ref, o_ref, ybuf_ref, sem_ref):
    i = pl.program_id(0)
    slot = i & 1
    tm = h_ref.shape[0]
    rows = _GatheredRows(yb_ref, ybuf_ref, sem_ref)

    @pl.when(i == 0)
    def _():
        rows.start(idx_ref, 0, slot)

    @pl.when(i + 1 < pl.num_programs(0))
    def _():
        rows.start(idxn_ref, 0, 1 - slot)

    rows.wait(slot)
    tg = tg_ref[...]
    f = tg[:, 0:1] * ybuf_ref[slot, pl.ds(0, tm), :]
    for kx in range(1, TOP_K):
        f = f + tg[:, kx:kx + 1] * ybuf_ref[slot, pl.ds(kx * tm, tm), :]
    y = h_ref[...] + f
    ms = jnp.mean(y * y, axis=-1, keepdims=True)
    o_ref[...] = y * lax.rsqrt(ms + RMS_EPS) * nf_ref[...]


def _final(h, yb, slot_of, tg, norm_f):
    t = h.shape[0]
    tm = TOK_TILE
    nt = t // tm
    idx = slot_of.reshape(nt, tm, TOP_K).transpose(0, 2, 1).reshape(nt * TOP_K * tm)
    row = lambda i: (i, 0)
    return pl.pallas_call(
        _final_kernel,
        grid=(nt,),
        in_specs=[pl.BlockSpec((TOP_K * tm,), lambda i: (i,), memory_space=pltpu.SMEM),
                  pl.BlockSpec((TOP_K * tm,), lambda i: (jnp.minimum(i + 1, nt - 1),), memory_space=pltpu.SMEM),
                  pl.BlockSpec((tm, D_MODEL), row), pl.BlockSpec(memory_space=pl.ANY),
                  pl.BlockSpec((tm, LANES), row), pl.BlockSpec((1, D_MODEL), lambda i: (0, 0))],
        out_specs=pl.BlockSpec((tm, D_MODEL), row),
        out_shape=jax.ShapeDtypeStruct((t, D_MODEL), F32),
        scratch_shapes=[pltpu.VMEM((2, TOP_K * tm, D_MODEL), yb.dtype), pltpu.SemaphoreType.DMA((2,))],
        compiler_params=_cparams(("arbitrary",)),
        name="final",
    )(idx, idx, h, yb, tg, norm_f.reshape(1, D_MODEL))


def _routing(top_e, bm):
    t = top_e.shape[0]
    a = t * TOP_K
    e_flat = top_e.reshape(a)
    order = jnp.argsort(e_flat).astype(jnp.int32)
    e_sorted = e_flat[order]
    counts = jnp.bincount(e_flat, length=N_EXPERTS).astype(jnp.int32)
    starts = jnp.cumsum(counts) - counts
    padded = (counts + bm - 1) // bm * bm
    pends = jnp.cumsum(padded)
    pstarts = pends - padded
    slot_sorted = (pstarts[e_sorted] + jnp.arange(a, dtype=jnp.int32) - starts[e_sorted]).astype(jnp.int32)
    n_blocks = _round_up(-(-a // bm) + N_EXPERTS, IDX_BLOCKS)
    p = n_blocks * bm
    blk_exp = jnp.minimum(jnp.sum(jnp.arange(n_blocks)[:, None] * bm >= pends[None, :], axis=1),
                          N_EXPERTS - 1).astype(jnp.int32)
    slot = jnp.arange(p, dtype=jnp.int32)
    e_slot = jnp.repeat(blk_exp, bm)
    j = slot - pstarts[e_slot]
    src = jnp.clip(starts[e_slot] + j, 0, a - 1)
    slot_tok = jnp.where((j < counts[e_slot]) & (slot < pends[-1]), order[src] // TOP_K, 0).astype(jnp.int32)
    slot_of = slot_sorted[jnp.argsort(order)].reshape(t, TOP_K)
    n_used = (pends[-1:] // bm).astype(jnp.int32)
    return slot_tok, slot_of, blk_exp, n_used


def kernel(x_prompt, x_sample, cache_kv, cache_win, state_wkv, state_shift, page_table, norm1, w_in, cmp_pe, cmp_mix, cmp_w1, cmp_w2, rw_mu, rw_w0, rw_w2, rw_a0, rw_a2, rw_g2, rw_kk, rw_ka, rw_rk, rw_ln_w, rw_ln_b, w_pa, w_pb, w_o, norm2, w_router, b_router, w_up, b_up, w_down, b_down, norm_f):
    assert w_in.shape[0] == 1, "single-layer trunk"
    b, s, _ = x_prompt.shape
    db, ds, _ = x_sample.shape
    n_pages = page_table.shape[1]
    assert cache_kv.shape[2] == PAGE
    past = n_pages * PAGE
    nbuf = cache_win.shape[2]
    tp, ts = b * s, db * ds
    t_real = tp + ts
    t_pad = _round_up(t_real, TOK_ALIGN)

    wi = w_in[0]
    w_pad = jnp.concatenate(
        [wi[:, :OFF_GN], jnp.pad(wi[:, OFF_GN:OFF_RW], ((0, 0), (0, GN_PAD - 3 * NSA_HEADS))), wi[:, OFF_RW:]],
        axis=1).astype(BF16)
    tile_g = lambda p: jnp.tile(p, (1, 1, NSA_KV_HEADS))
    mix2, pe2 = tile_g(cmp_mix[0].astype(F32)), tile_g(cmp_pe[0].astype(F32))
    bd = lambda w: jnp.stack([jnp.kron(jnp.eye(NSA_KV_HEADS, dtype=F32), w[i].astype(F32)) for i in range(2)])
    w1bd, w2bd = bd(cmp_w1[0]), bd(cmp_w2[0])
    zl = jnp.zeros((RW_DECAY_LORA, RW_WIDTH), F32)
    wlora = jnp.concatenate([jnp.concatenate([rw_w2[0], zl], axis=1),
                             jnp.concatenate([zl, rw_a2[0]], axis=1)], axis=0).astype(BF16)
    vec = lambda p: p[0].reshape(1, -1).astype(F32)
    wr_pad = jnp.pad(w_router[0].astype(F32), ((0, 0), (0, LANES - N_EXPERTS)))
    br_pad = jnp.concatenate([b_router[0].astype(F32), jnp.full((LANES - N_EXPERTS,), NEG_INF, F32)]).reshape(1, LANES)

    x_all = jnp.concatenate([x_prompt.reshape(tp, D_MODEL), x_sample.reshape(ts, D_MODEL)], axis=0)
    x_all = jnp.pad(x_all, ((0, t_pad - t_real), (0, 0)))
    q_all, kv_all, kvb_all, gn_all, rw_all, mg_all = _proj(x_all, norm1[0], w_pad)

    kv_p = kv_all[:tp].reshape(b, s, 6 * KV_LANES)
    kv_s = kv_all[tp:t_real].reshape(db, ds, 6 * KV_LANES)
    kvb_p = kvb_all[:tp].reshape(b, s, 6 * KV_LANES)

    tq = NSA_TQ
    assert s % tq == 0 and ds <= PAGE
    kc_p, vc_p = _compress(kv_p, mix2, pe2, w1bd, w2bd)
    o_nsa_p = _nsa(_block_diag_qt(q_all[:tp], b, s, tq), _gate_cols(gn_all[:tp], b, s, tq), kc_p, vc_p,
                   kvb_p, 2, kvb_p, 4, tq=tq, q_start=0, kw_start=0, ns_true=s // SEL_BLOCK)

    row_w = 4 * KV_LANES
    tail = jnp.pad(kv_s[:, :, :row_w], ((0, 0), (0, PAGE - ds), (0, 0)))
    cache2d = cache_kv[0].reshape(-1, PAGE, row_w)
    kc_s, vc_s = _compress_paged(page_table, cache2d, tail, mix2, pe2, w1bd, w2bd)
    kw_full = jnp.concatenate([cache_win[0].reshape(db, nbuf, 2 * KV_LANES), kv_s[:, :, row_w:]], axis=1)
    lw = _round_up(nbuf + ds, KEY_TILE)
    kw_pad = jnp.pad(kw_full, ((0, 0), (0, lw - (nbuf + ds)), (0, 0)))
    lp = _round_up(past + ds, SEL_BLOCK)
    tqs = NSA_TQ_FEW if ds <= NSA_TQ_FEW else NSA_TQ
    assert ds <= tqs
    pad_q = lambda x: jnp.pad(x.reshape(db, ds, -1), ((0, 0), (0, tqs - ds), (0, 0))).reshape(db * tqs, -1)
    o_nsa_s = _nsa_paged(_block_diag_qt(pad_q(q_all[tp:t_real]), db, tqs, tqs),
                         _gate_cols(pad_q(gn_all[tp:t_real]), db, tqs, tqs), kc_s, vc_s, page_table, cache2d, tail,
                         kw_pad, tq=tqs, q_start=past, kw_start=past - nbuf, ns_true=lp // SEL_BLOCK)
    o_nsa = jnp.concatenate([_own_head_rows(o_nsa_p, b, s, tq),
                             _own_head_rows(o_nsa_s, db, tqs, tqs).reshape(db, tqs, NSA_WIDTH)[:, :ds].reshape(ts, NSA_WIDTH),
                             jnp.zeros((t_pad - t_real, NSA_WIDTH), BF16)], axis=0)

    rw_p = rw_all[:tp].reshape(b, s, RW_COLS)
    rw_s = rw_all[tp:t_real].reshape(db, ds, RW_COLS)
    prev_p = jnp.concatenate([jnp.zeros((b, 1, RW_COLS), F32), rw_p[:, :-1]], axis=1)
    prev_s = jnp.concatenate([state_shift[0][:, None].astype(F32), rw_s[:, :-1]], axis=1)
    prev_all = jnp.concatenate([prev_p.reshape(tp, RW_COLS), prev_s.reshape(ts, RW_COLS),
                                jnp.zeros((t_pad - t_real, RW_COLS), F32)], axis=0)
    assert tp % TOK_ALIGN == 0
    pre_args = (vec(rw_mu), vec(rw_w0), vec(rw_a0), vec(rw_kk), vec(rw_ka), wlora, rw_g2[0].astype(BF16))
    pre_p = _rwkv_pre(rw_all, prev_all, 0, tp, *pre_args)
    pre_s = _rwkv_pre(rw_all, prev_all, tp, _round_up(ts, TOK_TILE), *pre_args)
    o_rw_p, wkv_p = _rwkv_group(pre_p, b, s, jnp.zeros((b, RW_HEADS, RW_HEAD_DIM, RW_HEAD_DIM), F32),
                                rw_ln_w[0], rw_ln_b[0], rw_rk[0])
    o_rw_s, wkv_s = _rwkv_group([x[:ts] for x in pre_s], db, ds, state_wkv[0],
                                rw_ln_w[0], rw_ln_b[0], rw_rk[0])
    o_rw = jnp.concatenate([o_rw_p, o_rw_s, jnp.zeros((t_pad - t_real, RW_WIDTH), F32)], axis=0)

    h, xn2, te, tg = _merge(x_all, o_nsa, o_rw, mg_all, w_pa[0].astype(BF16), w_pb[0].astype(BF16),
                            w_o[0].astype(BF16), norm2[0], wr_pad, br_pad)

    slot_tok, slot_of, blk_exp, n_used = _routing(te[:, :TOP_K], MOE_BM)
    yb = _experts(blk_exp, n_used, slot_tok, xn2, w_up[0].astype(BF16), b_up[0].astype(F32), w_down[0].astype(BF16),
                  b_down[0].astype(F32))
    y_all = _final(h, yb, slot_of, tg, norm_f)

    y_prompt = y_all[:tp].reshape(b, s, D_MODEL)
    y_sample = y_all[tp:t_real].reshape(db, ds, D_MODEL)
    kv_shape = (4, NSA_KV_HEADS, HEAD_DIM)
    win_shape = (2, NSA_KV_HEADS, HEAD_DIM)
    kv_prompt = kv_p[:, :, :row_w].reshape((1, b, s) + kv_shape)
    kv_sample = kv_s[:, :, :row_w].reshape((1, db, ds) + kv_shape)
    keep_p = min(WINDOW, s)
    win_prompt = kv_p[:, s - keep_p:, row_w:].reshape((1, b, keep_p) + win_shape)
    win_sample = kw_full[:, nbuf + ds - nbuf:].reshape((1, db, nbuf) + win_shape)
    shift_prompt = rw_p[:, -1][None]
    shift_sample = rw_s[:, -1][None]
    return (y_prompt, y_sample, kv_prompt, kv_sample, win_prompt, win_sample,
            wkv_p[None], wkv_s[None], shift_prompt, shift_sample)
```

```python
import functools
import math

import jax
import jax.numpy as jnp
import numpy as np
from jax import lax
from jax.experimental import pallas as pl
from jax.experimental.pallas import tpu as pltpu

F32 = jnp.float32
BF16 = jnp.bfloat16
HI = lax.Precision.HIGHEST

D_MODEL = 1024
NSA_HEADS = 8
NSA_KV_HEADS = 2
HEAD_DIM = 64
GROUP = NSA_HEADS // NSA_KV_HEADS
NSA_WIDTH = NSA_HEADS * HEAD_DIM
KV_LANES = NSA_KV_HEADS * HEAD_DIM
CMP_BLOCK = 32
CMP_STRIDE = 16
SEL_BLOCK = 64
SEL_TOP = 16
WINDOW = 512
SEL_FORCE = 1e9
NEG_INF = -1e30
SEL_MASK_BIG = 2.0 ** 100
PAGE = 128

RW_HEADS = 8
RW_HEAD_DIM = 64
RW_WIDTH = RW_HEADS * RW_HEAD_DIM
RW_DECAY_LORA = 64
RW_AAA_LORA = 64
RW_GATE_LORA = 128
RW_COLS = 3 * RW_WIDTH + RW_DECAY_LORA + RW_AAA_LORA + RW_GATE_LORA
GN_EPS = 64e-5

N_EXPERTS = 32
TOP_K = 4
D_FF = 1024
SWIGLU_LIMIT = 7.0
SWIGLU_ALPHA = 1.702
RMS_EPS = 1e-5

OFF_KV = NSA_WIDTH
OFF_GN = OFF_KV + 6 * KV_LANES
OFF_RW = OFF_GN + 3 * NSA_HEADS
OFF_MG = OFF_RW + RW_COLS
N_PROJ = OFF_MG + 2 * D_MODEL

LANES = 128
GN_PAD = LANES
P_Q = 0
P_KV = P_Q + NSA_WIDTH
P_GN = P_KV + 6 * KV_LANES
P_RW = P_GN + GN_PAD
P_MG = P_RW + RW_COLS
P_END = P_MG + 2 * D_MODEL

VMEM_LIMIT = 52 * 1024 * 1024

TOK_TILE = 256
TOK_ALIGN = 512
MOE_BM = 512
IDX_BLOCKS = 2
KEY_TILE = 128
NSA_TQ = 256
NSA_TQ_FEW = 16
GATE_ROWS = 8


def _cparams(sem):
    return pltpu.CompilerParams(dimension_semantics=sem, vmem_limit_bytes=VMEM_LIMIT)


def _sigmoid(x):
    return 1.0 / (1.0 + jnp.exp(-x))


def _round_up(a, b):
    return -(-a // b) * b


def _proj_kernel(x_ref, g_ref, w_ref, q_ref, kv_ref, kvb_ref, gn_ref, rw_ref, mg_ref):
    x = x_ref[...]
    ms = jnp.mean(x * x, axis=-1, keepdims=True)
    xn = (x * lax.rsqrt(ms + RMS_EPS) * g_ref[...]).astype(BF16)

    def seg(a, b):
        return jnp.dot(xn, w_ref[:, a:b], preferred_element_type=F32)

    q_ref[...] = (seg(P_Q, P_KV) * (HEAD_DIM ** -0.5)).astype(BF16)
    kv = seg(P_KV, P_GN)
    kv_ref[...] = kv
    kvb_ref[...] = kv.astype(BF16)
    gn_ref[...] = _sigmoid(seg(P_GN, P_RW))
    rw_ref[...] = seg(P_RW, P_MG)
    mg_ref[...] = _sigmoid(seg(P_MG, P_END))


def _proj(x, norm1, w_pad):
    T = x.shape[0]
    tm = TOK_TILE
    row = lambda i: (i, 0)
    const = lambda i: (0, 0)
    outs = (
        jax.ShapeDtypeStruct((T, NSA_WIDTH), BF16),
        jax.ShapeDtypeStruct((T, 6 * KV_LANES), F32),
        jax.ShapeDtypeStruct((T, 6 * KV_LANES), BF16),
        jax.ShapeDtypeStruct((T, GN_PAD), F32),
        jax.ShapeDtypeStruct((T, RW_COLS), F32),
        jax.ShapeDtypeStruct((T, 2 * D_MODEL), F32),
    )
    return pl.pallas_call(
        _proj_kernel,
        grid=(T // tm,),
        in_specs=[pl.BlockSpec((tm, D_MODEL), row), pl.BlockSpec((1, D_MODEL), const),
                  pl.BlockSpec((D_MODEL, P_END), const)],
        out_specs=[pl.BlockSpec((tm, o.shape[1]), row) for o in outs],
        out_shape=outs,
        compiler_params=_cparams(("parallel",)),
        name="proj",
    )(x, norm1.reshape(1, D_MODEL), w_pad)


class _PagedRows:
    def __init__(self, pt_ref, cache_ref, tail_ref, buf_refs, sem_ref, col0):
        self.pt, self.cache, self.tail, self.bufs, self.sem, self.col0 = pt_ref, cache_ref, tail_ref, buf_refs, sem_ref, col0
        self.n_pages = pt_ref.shape[1]

    def _copies(self, src, row0, slot):
        return [pltpu.make_async_copy(src.at[:, pl.ds((self.col0 + i) * KV_LANES, KV_LANES)],
                                      buf.at[slot, pl.ds(row0, PAGE), :], self.sem.at[slot])
                for i, buf in enumerate(self.bufs)]

    def _page(self, b, p, slot):
        return self._copies(self.cache.at[self.pt[b, p]], p * PAGE, slot)

    def _tail(self, b, slot):
        return self._copies(self.tail.at[b], self.n_pages * PAGE, slot)

    def start(self, b, slot):
        def body(p, c):
            for cp in self._page(b, p, slot):
                cp.start()
            return c
        lax.fori_loop(0, self.n_pages, body, 0)
        for cp in self._tail(b, slot):
            cp.start()

    def wait(self, b, slot):
        def body(p, c):
            for cp in self._page(b, p, slot):
                cp.wait()
            return c
        lax.fori_loop(0, self.n_pages, body, 0)
        for cp in self._tail(b, slot):
            cp.wait()

    def fetch(self):
        b = pl.program_id(0)
        slot = b & 1

        @pl.when(b == 0)
        def _():
            self.start(b, slot)

        @pl.when(b + 1 < pl.num_programs(0))
        def _():
            self.start(b + 1, 1 - slot)

        self.wait(b, slot)
        return slot

    def half(self, slot, i):
        return self.bufs[i].at[pl.ds(slot, 1)]


def _gelu_tanh(x):
    return 0.5 * x * (1.0 + jnp.tanh(math.sqrt(2.0 / math.pi) * (x + 0.044715 * (x * x * x))))


def _compress_paged_kernel(pt_ref, cache_ref, tail_ref, mix_ref, pe_ref, w1_ref, w2_ref, kc_ref, vc_ref,
                           kbuf_ref, vbuf_ref, sem_ref):
    rows = _PagedRows(pt_ref, cache_ref, tail_ref, (kbuf_ref, vbuf_ref), sem_ref, 0)
    slot = rows.fetch()
    _compress_kernel(rows.half(slot, 0), rows.half(slot, 1), mix_ref, pe_ref, w1_ref, w2_ref, kc_ref, vc_ref)


def _compress_paged(page_table, cache2d, tail, mix2, pe2, w1bd, w2bd):
    db, n_pages = page_table.shape
    l = (n_pages + 1) * PAGE
    nch = _round_up(l // CMP_STRIDE, LANES)
    const3 = lambda i, pt: (0, 0, 0)
    any_spec = pl.BlockSpec(memory_space=pl.ANY)
    out = jax.ShapeDtypeStruct((db, nch, LANES), F32)
    return pl.pallas_call(
        _compress_paged_kernel,
        grid_spec=pltpu.PrefetchScalarGridSpec(
            num_scalar_prefetch=1, grid=(db,),
            in_specs=[any_spec, any_spec,
                      pl.BlockSpec((2, CMP_BLOCK, LANES), const3), pl.BlockSpec((2, CMP_BLOCK, LANES), const3),
                      pl.BlockSpec((2, LANES, LANES), const3), pl.BlockSpec((2, LANES, LANES), const3)],
            out_specs=[pl.BlockSpec((1, nch, LANES), lambda i, pt: (i, 0, 0))] * 2,
            scratch_shapes=[pltpu.VMEM((2, l, KV_LANES), cache2d.dtype)] * 2 + [pltpu.SemaphoreType.DMA((2,))]),
        out_shape=(out, out),
        compiler_params=_cparams(("arbitrary",)),
        name="compress_paged",
    )(page_table, cache2d, tail, mix2, pe2, w1bd, w2bd)


def _compress_kernel(rk_ref, rv_ref, mix_ref, pe_ref, w1_ref, w2_ref, kc_ref, vc_ref):
    nch = rk_ref.shape[1] // CMP_STRIDE
    nch_pad = kc_ref.shape[1]
    for idx, (r_ref, o_ref) in enumerate(((rk_ref, kc_ref), (rv_ref, vc_ref))):
        mix = mix_ref[idx]
        cs0 = jnp.zeros((nch, LANES), F32)
        cs1 = jnp.zeros((nch, LANES), F32)
        for p in range(CMP_STRIDE):
            xp = r_ref[0, pl.ds(p, nch, stride=CMP_STRIDE), :].astype(F32)
            cs0 = cs0 + xp * mix[p:p + 1]
            cs1 = cs1 + xp * mix[CMP_STRIDE + p:CMP_STRIDE + p + 1]
        c0 = jnp.sum(mix * pe_ref[idx], axis=0, keepdims=True)
        pre = c0 + cs0 + pltpu.roll(cs1, nch - 1, 0)
        h = _gelu_tanh(jnp.dot(pre, w1_ref[idx], precision=HI, preferred_element_type=F32))
        o_ref[0, pl.ds(0, nch), :] = jnp.dot(h, w2_ref[idx], precision=HI, preferred_element_type=F32)
        if nch_pad > nch:
            o_ref[0, pl.ds(nch, nch_pad - nch), :] = jnp.zeros((nch_pad - nch, LANES), F32)


def _compress(rows3d, mix2, pe2, w1bd, w2bd):
    b, l, _ = rows3d.shape
    nch = _round_up(l // CMP_STRIDE, LANES)
    const3 = lambda i: (0, 0, 0)
    out = jax.ShapeDtypeStruct((b, nch, LANES), F32)
    return pl.pallas_call(
        _compress_kernel,
        grid=(b,),
        in_specs=[pl.BlockSpec((1, l, LANES), lambda i: (i, 0, 0)),
                  pl.BlockSpec((1, l, LANES), lambda i: (i, 0, 1)),
                  pl.BlockSpec((2, CMP_BLOCK, LANES), const3), pl.BlockSpec((2, CMP_BLOCK, LANES), const3),
                  pl.BlockSpec((2, LANES, LANES), const3), pl.BlockSpec((2, LANES, LANES), const3)],
        out_specs=[pl.BlockSpec((1, nch, LANES), lambda i: (i, 0, 0))] * 2,
        out_shape=(out, out),
        compiler_params=_cparams(("parallel",)),
        name="compress",
    )(rows3d, rows3d, mix2, pe2, w1bd, w2bd)


def _masked_softmax0(s, mask):
    s = jnp.where(mask, s, NEG_INF)
    m = jnp.max(s, axis=0, keepdims=True)
    e = jnp.where(mask, jnp.exp(s - m), 0.0)
    return e / jnp.maximum(jnp.sum(e, axis=0, keepdims=True), 1e-30)


def _flash_t(q_aug, k_ref, v_ref, lo, hi, k_aug_fn, mask_fn, kt_rows, zero_masked):
    r = q_aug.shape[1]

    def body(kt, carry):
        m, l, acc = carry
        off = pl.multiple_of(kt * kt_rows, kt_rows)
        k = k_aug_fn(kt, k_ref[0, pl.ds(off, kt_rows), :].astype(BF16))
        vt = v_ref[0, pl.ds(off, kt_rows), :].astype(F32).T.astype(BF16)
        s = jnp.dot(k, q_aug, preferred_element_type=F32)
        mask = mask_fn(kt)
        s = jnp.where(mask, s, NEG_INF)
        m_new = jnp.maximum(m, jnp.max(s, axis=0, keepdims=True))
        p = jnp.exp(s - m_new)
        if zero_masked:
            p = jnp.where(mask, p, 0.0)
        alpha = jnp.exp(m - m_new)
        l = alpha * l + jnp.sum(p, axis=0, keepdims=True)
        acc = alpha * acc + jnp.dot(vt, p.astype(BF16), preferred_element_type=F32)
        return m_new, l, acc

    init = (jnp.full((1, r), NEG_INF, F32), jnp.zeros((1, r), F32), jnp.zeros((LANES, r), F32))
    _, l, acc = lax.fori_loop(lo, hi, body, init)
    return acc / jnp.maximum(l, 1e-30)


def _nsa_kernel(*refs, **static):
    _nsa_body(pl.program_id(1), *refs, **static)


def _nsa_paged_kernel(pt_ref, qt_ref, gt_ref, kc_ref, vc_ref, cover_ref, cache_ref, tail_ref, kw_ref, vw_ref,
                      o_ref, kbuf_ref, vbuf_ref, sem_ref, **static):
    rows = _PagedRows(pt_ref, cache_ref, tail_ref, (kbuf_ref, vbuf_ref), sem_ref, 2)
    slot = rows.fetch()
    _nsa_body(0, qt_ref, gt_ref, kc_ref, vc_ref, cover_ref, rows.half(slot, 0), rows.half(slot, 1), kw_ref, vw_ref,
              o_ref, **static)


def _nsa_body(qi, qt_ref, gt_ref, kc_ref, vc_ref, cover_ref, ks_ref, vs_ref, kw_ref, vw_ref, o_ref,
              *, tq, q_start, kw_start, n_top, kt_sel, kt_win):
    r = NSA_HEADS * tq
    t0 = q_start + qi * tq
    qt = qt_ref[0, 0]
    t_col = t0 + (lax.broadcasted_iota(jnp.int32, (1, r), 1) & (tq - 1))

    kc = kc_ref[0].astype(BF16)
    nch = kc.shape[0]
    s_c = jnp.dot(kc, qt, preferred_element_type=F32)
    cmp_end = lax.broadcasted_iota(jnp.int32, (nch, 1), 0) * CMP_STRIDE + (CMP_BLOCK - 1)
    p_c = _masked_softmax0(s_c, cmp_end <= t_col)
    o_c = jnp.dot(vc_ref[0].T.astype(BF16), p_c.astype(BF16), preferred_element_type=F32)

    ns_pad = cover_ref.shape[0]
    aligned = tq % LANES == 0
    lane_r = lax.broadcasted_iota(jnp.int32, (1, r), 1)
    cur = ((t0 + lax.broadcasted_iota(jnp.int32, (1, tq), 1)) if aligned else t_col) >> 6
    jb = lax.broadcasted_iota(jnp.int32, (ns_pad, 1), 0)
    jbf = jb.astype(F32)
    elig = jb <= cur
    forced = (jb == 0) | (jb == cur) | (jb == cur - 1)
    notsel = []
    for g in range(NSA_KV_HEADS):
        base = g * GROUP * tq
        if aligned:
            pg = p_c[:, base:base + tq]
            for n in range(1, GROUP):
                pg = pg + p_c[:, base + n * tq:base + (n + 1) * tq]
        else:
            pg = p_c
            for n in range(1, GROUP):
                pg = pg + pltpu.roll(p_c, r - n * tq, 1)
        imp = jnp.dot(cover_ref[...], pg, precision=HI, preferred_element_type=F32)
        score = jnp.where(elig, jnp.where(forced, SEL_FORCE, imp), -SEL_FORCE)
        sel = jnp.zeros(score.shape, F32)
        for _ in range(n_top):
            m = jnp.max(score, axis=0, keepdims=True)
            first = jnp.min(jnp.where(score == m, jbf, 1e9), axis=0, keepdims=True)
            hit = jbf == first
            sel = jnp.where(hit, 1.0, sel)
            score = jnp.where(hit, -3e38, score)
        ns = jnp.where(elig, 1.0 - sel, 1.0)
        if not aligned:
            ns = jnp.where((lane_r >= base) & (lane_r < base + tq), ns, 0.0)
            rep = ns
            for n in range(1, GROUP):
                rep = rep + pltpu.roll(ns, n * tq, 1)
            ns = rep
        notsel.append(ns)
    if aligned:
        notsel_cols = jnp.concatenate([notsel[0]] * GROUP + [notsel[1]] * GROUP, axis=1).astype(BF16)
    else:
        notsel_cols = (notsel[0] + notsel[1]).astype(BF16)

    q_aug = jnp.concatenate([qt, notsel_cols], axis=0)

    def sel_keys(kt, k):
        blk = kt * (kt_sel // SEL_BLOCK) + (lax.broadcasted_iota(jnp.int32, (kt_sel, ns_pad), 0) >> 6)
        tag = jnp.where(lax.broadcasted_iota(jnp.int32, (kt_sel, ns_pad), 1) == blk, -SEL_MASK_BIG, 0.0)
        return jnp.concatenate([k, tag.astype(BF16)], axis=1)

    def sel_mask(kt):
        kpos = kt * kt_sel + lax.broadcasted_iota(jnp.int32, (kt_sel, 1), 0)
        return kpos <= t_col

    hi_s = (t0 + tq - 1) // kt_sel + 1
    o_s = _flash_t(q_aug, ks_ref, vs_ref, 0, hi_s, sel_keys, sel_mask, kt_sel, zero_masked=False)

    def win_mask(kt):
        kpos = kw_start + kt * kt_win + lax.broadcasted_iota(jnp.int32, (kt_win, 1), 0)
        dist = t_col - kpos
        return (dist >= 0) & (dist <= WINDOW)

    n_wt = kw_ref.shape[1] // kt_win
    lo_w = jnp.maximum(t0 - WINDOW - kw_start, 0) // kt_win
    hi_w = jnp.minimum((t0 + tq - 1 - kw_start) // kt_win + 1, n_wt)
    o_w = _flash_t(qt, kw_ref, vw_ref, lo_w, hi_w, lambda kt, k: k, win_mask, kt_win, zero_masked=True)

    gt = gt_ref[0, 0]
    out = gt[0:1] * o_c + gt[1:2] * o_s + gt[2:3] * o_w
    o_ref[0, 0] = out.astype(o_ref.dtype)


def _cover_matrix_t(nch, ns_pad):
    ci = np.arange(nch)[None, :] * CMP_STRIDE
    sj = np.arange(ns_pad)[:, None] * SEL_BLOCK
    return jnp.asarray(((ci < sj + SEL_BLOCK) & (ci + CMP_BLOCK > sj)).astype(np.float32))


def _block_diag_qt(q, b, sq, tq):
    nqt = sq // tq
    qr = q.reshape(b, nqt, tq, NSA_KV_HEADS, GROUP, HEAD_DIM).transpose(0, 1, 3, 5, 4, 2)
    z = jnp.zeros_like(qr[:, :, 0])
    top = jnp.stack([qr[:, :, 0], z], axis=3)
    bot = jnp.stack([z, qr[:, :, 1]], axis=3)
    return jnp.stack([top, bot], axis=2).reshape(b, nqt, LANES, NSA_HEADS * tq)


def _nsa_specs(nch, ns_pad, lw, tq, kw_col, idx):
    r = NSA_HEADS * tq
    return dict(
        qt=pl.BlockSpec((1, 1, LANES, r), idx(lambda i, j: (i, j, 0, 0))),
        gt=pl.BlockSpec((1, 1, GATE_ROWS, r), idx(lambda i, j: (i, j, 0, 0))),
        kc=pl.BlockSpec((1, nch, LANES), idx(lambda i, j: (i, 0, 0))),
        cover=pl.BlockSpec((ns_pad, nch), idx(lambda i, j: (0, 0))),
        kw=pl.BlockSpec((1, lw, LANES), idx(lambda i, j: (i, 0, kw_col))),
        vw=pl.BlockSpec((1, lw, LANES), idx(lambda i, j: (i, 0, kw_col + 1))),
        out=pl.BlockSpec((1, 1, LANES, r), idx(lambda i, j: (i, j, 0, 0))))


def _gate_cols(gn, b, sq, tq):
    nqt = sq // tq
    g = gn[:, :3 * NSA_HEADS].reshape(b, nqt, tq, 3, NSA_HEADS).transpose(0, 1, 3, 4, 2)
    return jnp.pad(g.reshape(b, nqt, 3, NSA_HEADS * tq), ((0, 0), (0, 0), (0, GATE_ROWS - 3), (0, 0)))


def _own_head_rows(o, b, sq, tq):
    nqt = sq // tq
    o = o.reshape(b, nqt, NSA_KV_HEADS, HEAD_DIM, NSA_KV_HEADS, GROUP, tq)
    own = jnp.stack([o[:, :, g, :, g] for g in range(NSA_KV_HEADS)], axis=2)
    return own.transpose(0, 1, 5, 2, 4, 3).reshape(b * sq, NSA_WIDTH)


def _key_tile(rows, most):
    for c in range(most, 0, -1):
        if rows % (c * KEY_TILE) == 0:
            return c * KEY_TILE
    raise ValueError("row count must be a multiple of the key tile")


def _nsa(qt, gt, kc, vc, ksv, ks_col, kwv, kw_col, *, tq, q_start, kw_start, ns_true):
    b, nqt = qt.shape[:2]
    nch = kc.shape[1]
    lk = ksv.shape[1]
    ns_pad = _round_up(lk // SEL_BLOCK, LANES)
    sp = _nsa_specs(nch, ns_pad, kwv.shape[1], tq, kw_col, lambda f: f)
    kern = functools.partial(_nsa_kernel, tq=tq, q_start=q_start, kw_start=kw_start, n_top=min(SEL_TOP, ns_true),
                             kt_sel=_key_tile(lk, 4), kt_win=_key_tile(kwv.shape[1], 2))
    return pl.pallas_call(
        kern,
        grid=(b, nqt),
        in_specs=[sp["qt"], sp["gt"], sp["kc"], sp["kc"], sp["cover"],
                  pl.BlockSpec((1, lk, LANES), lambda i, j: (i, 0, ks_col)),
                  pl.BlockSpec((1, lk, LANES), lambda i, j: (i, 0, ks_col + 1)),
                  sp["kw"], sp["vw"]],
        out_specs=sp["out"],
        out_shape=jax.ShapeDtypeStruct((b, nqt, LANES, NSA_HEADS * tq), BF16),
        compiler_params=_cparams(("parallel", "arbitrary")),
        name="nsa",
    )(qt, gt, kc, vc, _cover_matrix_t(nch, ns_pad), ksv, ksv, kwv, kwv)


def _nsa_paged(qt, gt, kc, vc, page_table, cache2d, tail, kwv, *, tq, q_start, kw_start, ns_true):
    db, nqt = qt.shape[:2]
    assert nqt == 1
    nch = kc.shape[1]
    lk = (page_table.shape[1] + 1) * PAGE
    ns_pad = _round_up(lk // SEL_BLOCK, LANES)
    sp = _nsa_specs(nch, ns_pad, kwv.shape[1], tq, 0, lambda f: (lambda i, pt: f(i, 0)))
    kern = functools.partial(_nsa_paged_kernel, tq=tq, q_start=q_start, kw_start=kw_start, n_top=min(SEL_TOP, ns_true),
                             kt_sel=_key_tile(lk, 5), kt_win=_key_tile(kwv.shape[1], 5))
    any_spec = pl.BlockSpec(memory_space=pl.ANY)
    return pl.pallas_call(
        kern,
        grid_spec=pltpu.PrefetchScalarGridSpec(
            num_scalar_prefetch=1, grid=(db,),
            in_specs=[sp["qt"], sp["gt"], sp["kc"], sp["kc"], sp["cover"], any_spec, any_spec, sp["kw"], sp["vw"]],
            out_specs=sp["out"],
            scratch_shapes=[pltpu.VMEM((2, lk, KV_LANES), cache2d.dtype)] * 2 + [pltpu.SemaphoreType.DMA((2,))]),
        out_shape=jax.ShapeDtypeStruct((db, 1, LANES, NSA_HEADS * tq), BF16),
        compiler_params=_cparams(("arbitrary",)),
        name="nsa_paged",
    )(page_table, qt, gt, kc, vc, _cover_matrix_t(nch, ns_pad), cache2d, tail, kwv, kwv)


def _rwkv_pre_kernel(c_ref, p_ref, mu_ref, w0_ref, a0_ref, kkw_ref, ka_ref, wl_ref, g2_ref,
                     r_ref, k_ref, v_ref, kk_ref, a_ref, d_ref, g_ref):
    cols = c_ref[...]
    xs = cols + mu_ref[...] * (p_ref[...] - cols)
    o1, o2, o3 = RW_WIDTH, 2 * RW_WIDTH, 3 * RW_WIDTH
    o5 = o3 + RW_DECAY_LORA + RW_AAA_LORA
    k = xs[:, o1:o2]
    wa = xs[:, o3:o5]
    lane = lax.broadcasted_iota(jnp.int32, wa.shape, 1)
    wa = jnp.where(lane < RW_DECAY_LORA, jnp.tanh(wa), wa)
    lo = jnp.dot(wa.astype(BF16), wl_ref[...], preferred_element_type=F32)
    z = -(w0_ref[...] + lo[:, :RW_WIDTH])
    softplus = jnp.maximum(z, 0.0) + jnp.log(1.0 + jnp.exp(-jnp.abs(z)))
    w = -softplus - 0.5
    a = _sigmoid(a0_ref[...] + lo[:, RW_WIDTH:])
    r_ref[...] = xs[:, :o1]
    k_ref[...] = k * (1.0 + (a - 1.0) * ka_ref[...])
    v_ref[...] = xs[:, o2:o3]
    kk_ref[...] = k * kkw_ref[...]
    a_ref[...] = a
    d_ref[...] = jnp.exp(-jnp.exp(w))
    g_ref[...] = jnp.dot(_sigmoid(xs[:, o5:]).astype(BF16), g2_ref[...], preferred_element_type=F32)


def _rwkv_pre(cols, prev, row0, nrows, mu, w0, a0, kkw, ka, wlora, g2):
    tm = TOK_TILE
    assert row0 % tm == 0 and nrows % tm == 0 and row0 + nrows <= cols.shape[0]
    tile0 = row0 // tm
    row = lambda i: (i, 0)
    src = lambda i: (i + tile0, 0)
    const = lambda i: (0, 0)
    vec = pl.BlockSpec((1, RW_WIDTH), const)
    out = jax.ShapeDtypeStruct((nrows, RW_WIDTH), F32)
    return pl.pallas_call(
        _rwkv_pre_kernel,
        grid=(nrows // tm,),
        in_specs=[pl.BlockSpec((tm, RW_COLS), src), pl.BlockSpec((tm, RW_COLS), src),
                  pl.BlockSpec((1, RW_COLS), const), vec, vec, vec, vec,
                  pl.BlockSpec((RW_DECAY_LORA + RW_AAA_LORA, 2 * RW_WIDTH), const),
                  pl.BlockSpec((RW_GATE_LORA, RW_WIDTH), const)],
        out_specs=[pl.BlockSpec((tm, RW_WIDTH), row)] * 7,
        out_shape=(out,) * 7,
        compiler_params=_cparams(("parallel",)),
        name="rwkv_pre",
    )(cols, prev, mu, w0, a0, kkw, ka, wlora, g2)


def _rwkv_scan_kernel(r_ref, k_ref, v_ref, kk_ref, a_ref, d_ref, g_ref, s0_ref, lnw_ref, lnb_ref, rk_ref,
                      o_ref, sf_ref, st_ref, kkn_ref, b_ref):
    n = RW_HEAD_DIM
    tb = r_ref.shape[0]
    ti = pl.program_id(1)

    @pl.when(ti == 0)
    def _():
        st_ref[...] = s0_ref[...]

    def step(t, c):
        kk = kk_ref[t]
        nrm = jnp.sqrt(jnp.sum(kk * kk, axis=0, keepdims=True))
        kkn = kk / jnp.maximum(nrm, 1e-12)
        kkn_ref[...] = kkn
        b_ref[...] = kkn * a_ref[t]
        vv = v_ref[t]
        sa = jnp.zeros((n, LANES), F32)
        for i in range(n):
            sa = sa - st_ref[i] * kkn_ref[pl.ds(i, 1), :]
        y = jnp.zeros((n, LANES), F32)
        for i in range(n):
            s_new = (st_ref[i] * d_ref[t, pl.ds(i, 1), :] + sa * b_ref[pl.ds(i, 1), :]
                     + vv * k_ref[t, pl.ds(i, 1), :])
            st_ref[i] = s_new
            y = y + s_new * r_ref[t, pl.ds(i, 1), :]
        mu = jnp.mean(y, axis=0, keepdims=True)
        yc = y - mu
        var = jnp.mean(yc * yc, axis=0, keepdims=True)
        yn = yc * lax.rsqrt(var + GN_EPS) * lnw_ref[...] + lnb_ref[...]
        bonus = jnp.sum(r_ref[t] * k_ref[t] * rk_ref[...], axis=0, keepdims=True) * vv
        o_ref[t] = (yn + bonus) * g_ref[t]
        return c

    lax.fori_loop(0, tb, step, 0)

    @pl.when(ti == pl.num_programs(1) - 1)
    def _():
        sf_ref[...] = st_ref[...]


def _rwkv_scan(seqs, s0, lnw, lnb, rk, tb):
    s, n, nl = seqs[0].shape
    blk = pl.BlockSpec((tb, n, LANES), lambda j, i: (i, 0, j))
    st = pl.BlockSpec((n, n, LANES), lambda j, i: (0, 0, j))
    vec = pl.BlockSpec((n, LANES), lambda j, i: (0, j))
    return pl.pallas_call(
        _rwkv_scan_kernel,
        grid=(nl // LANES, s // tb),
        in_specs=[blk] * 7 + [st, vec, vec, vec],
        out_specs=[blk, st],
        out_shape=(jax.ShapeDtypeStruct((s, n, nl), F32), jax.ShapeDtypeStruct((n, n, nl), F32)),
        scratch_shapes=[pltpu.VMEM((n, n, LANES), F32), pltpu.VMEM((n, LANES), F32), pltpu.VMEM((n, LANES), F32)],
        compiler_params=_cparams(("parallel", "arbitrary")),
        name="rwkv_scan",
    )(*seqs, s0, lnw, lnb, rk)


def _to_scan_layout(x, b, s):
    return x.reshape(b, s, RW_HEADS, RW_HEAD_DIM).transpose(1, 3, 0, 2).reshape(s, RW_HEAD_DIM, b * RW_HEADS)


def _from_scan_layout(y, b, s):
    return y.reshape(s, RW_HEAD_DIM, b, RW_HEADS).transpose(2, 0, 3, 1).reshape(b * s, RW_WIDTH)


def _rwkv_group(pre, b, s, wkv0, lnw, lnb, rk):
    nl = b * RW_HEADS
    nlp = _round_up(nl, LANES)

    def padl(x):
        return x if nlp == nl else jnp.pad(x, [(0, 0)] * (x.ndim - 1) + [(0, nlp - nl)])

    seqs = [padl(_to_scan_layout(x, b, s)) for x in pre]
    s0 = padl(wkv0.astype(F32).transpose(3, 2, 0, 1).reshape(RW_HEAD_DIM, RW_HEAD_DIM, nl))
    tile = lambda p: padl(jnp.tile(p.reshape(RW_HEADS, RW_HEAD_DIM).T, (1, b)))
    tb = math.gcd(s, 32)
    o, sf = _rwkv_scan(seqs, s0, tile(lnw), tile(lnb), tile(rk), tb)
    o = _from_scan_layout(o[:, :, :nl], b, s)
    sf = sf[:, :, :nl].reshape(RW_HEAD_DIM, RW_HEAD_DIM, b, RW_HEADS).transpose(2, 3, 1, 0)
    return o, sf


def _merge_kernel(x_ref, on_ref, or_ref, mg_ref, wpa_ref, wpb_ref, wo_ref, n2_ref, wr_ref, br_ref,
                  h_ref, xn_ref, te_ref, tg_ref):
    pa = jnp.dot(on_ref[...], wpa_ref[...], preferred_element_type=F32)
    pb = jnp.dot(or_ref[...].astype(BF16), wpb_ref[...], preferred_element_type=F32)
    merged = mg_ref[:, :D_MODEL] * pa + mg_ref[:, D_MODEL:] * pb
    h = x_ref[...] + jnp.dot(merged.astype(BF16), wo_ref[...], preferred_element_type=F32)
    h_ref[...] = h
    ms = jnp.mean(h * h, axis=-1, keepdims=True)
    xn = h * lax.rsqrt(ms + RMS_EPS) * n2_ref[...]
    xn_ref[...] = xn
    logits = jnp.dot(xn, wr_ref[...], precision=HI, preferred_element_type=F32) + br_ref[...]
    lane = lax.broadcasted_iota(jnp.int32, logits.shape, 1)
    lanef = lane.astype(F32)
    te = jnp.zeros(logits.shape, F32)
    tv = jnp.full(logits.shape, NEG_INF, F32)
    for kx in range(TOP_K):
        m = jnp.max(logits, axis=-1, keepdims=True)
        first = jnp.min(jnp.where(logits == m, lanef, 1e9), axis=-1, keepdims=True)
        te = jnp.where(lane == kx, first, te)
        tv = jnp.where(lane == kx, m, tv)
        logits = jnp.where(lanef == first, -3e38, logits)
    e = jnp.where(lane < TOP_K, jnp.exp(tv - jnp.max(tv, axis=-1, keepdims=True)), 0.0)
    te_ref[...] = te.astype(jnp.int32)
    tg_ref[...] = e / jnp.sum(e, axis=-1, keepdims=True)


def _merge(x, o_nsa, o_rw, mg, wpa, wpb, wo, norm2, wr_pad, br_pad):
    t = x.shape[0]
    tm = TOK_TILE
    row = lambda i: (i, 0)
    const = lambda i: (0, 0)
    outs = (jax.ShapeDtypeStruct((t, D_MODEL), F32), jax.ShapeDtypeStruct((t, D_MODEL), F32),
            jax.ShapeDtypeStruct((t, LANES), jnp.int32), jax.ShapeDtypeStruct((t, LANES), F32))
    return pl.pallas_call(
        _merge_kernel,
        grid=(t // tm,),
        in_specs=[pl.BlockSpec((tm, D_MODEL), row), pl.BlockSpec((tm, NSA_WIDTH), row),
                  pl.BlockSpec((tm, RW_WIDTH), row), pl.BlockSpec((tm, 2 * D_MODEL), row),
                  pl.BlockSpec((NSA_WIDTH, D_MODEL), const), pl.BlockSpec((RW_WIDTH, D_MODEL), const),
                  pl.BlockSpec((D_MODEL, D_MODEL), const), pl.BlockSpec((1, D_MODEL), const),
                  pl.BlockSpec((D_MODEL, LANES), const), pl.BlockSpec((1, LANES), const)],
        out_specs=[pl.BlockSpec((tm, o.shape[1]), row) for o in outs],
        out_shape=outs,
        compiler_params=_cparams(("parallel",)),
        name="merge",
    )(x, o_nsa, o_rw, mg, wpa, wpb, wo, norm2.reshape(1, D_MODEL), wr_pad, br_pad)


class _GatheredRows:
    def __init__(self, src_ref, buf_ref, sem_ref):
        self.src, self.buf, self.sem = src_ref, buf_ref, sem_ref
        self.n = buf_ref.shape[1]

    def _row(self, idx_ref, off, j, slot):
        return pltpu.make_async_copy(self.src.at[pl.ds(idx_ref[off + j], 1)], self.buf.at[slot, pl.ds(j, 1)],
                                     self.sem.at[slot])

    def start(self, idx_ref, off, slot):
        def body(j2, c):
            self._row(idx_ref, off, 2 * j2, slot).start(priority=0)
            self._row(idx_ref, off, 2 * j2 + 1, slot).start(priority=1)
            return c
        lax.fori_loop(0, self.n // 2, body, 0, unroll=4)

    def start_unrolled(self, idx_ref, off, slot):
        for j in range(self.n):
            self._row(idx_ref, off, j, slot).start()

    def wait(self, slot):
        pltpu.make_async_copy(self.src.at[pl.ds(0, self.n)], self.buf.at[slot], self.sem.at[slot]).wait()


def _expert_kernel(be_ref, nu_ref, idx_ref, idxn_ref, xn_ref, wu_ref, bu_ref, wd_ref, bd_ref, o_ref, xbuf_ref, sem_ref):
    i = pl.program_id(0)
    n_used = nu_ref[0]
    slot = i & 1
    bm = xbuf_ref.shape[1]
    rows = _GatheredRows(xn_ref, xbuf_ref, sem_ref)

    @pl.when(i == 0)
    def _():
        rows.start(idx_ref, 0, slot)

    @pl.when(i < n_used)
    def _():
        rows.wait(slot)
        x = xbuf_ref[slot].astype(BF16)
        hu = jnp.dot(x, wu_ref[0], preferred_element_type=F32) + bu_ref[0]
        rows.start_unrolled(idxn_ref, (1 - slot) * bm, 1 - slot)
        gt = jnp.minimum(hu[:, :D_FF], SWIGLU_LIMIT)
        up = jnp.clip(hu[:, D_FF:], -SWIGLU_LIMIT, SWIGLU_LIMIT)
        hh = (up + 1.0) * gt * _sigmoid(SWIGLU_ALPHA * gt)
        o_ref[...] = jnp.dot(hh.astype(BF16), wd_ref[0], preferred_element_type=F32) + bd_ref[0]

    @pl.when(i + 1 == n_used)
    def _():
        rows.wait(1 - slot)

    @pl.when(i >= n_used)
    def _():
        o_ref[...] = jnp.zeros(o_ref.shape, o_ref.dtype)


def _experts(blk_exp, n_used, slot_tok, xn, w_up, b_up, w_down, b_down):
    p = slot_tok.shape[0]
    bm = MOE_BM
    nb = p // bm
    assert nb % 2 == 0
    wmap = lambda i, be, nu: (be[i], 0, 0)
    return pl.pallas_call(
        _expert_kernel,
        grid_spec=pltpu.PrefetchScalarGridSpec(
            num_scalar_prefetch=2, grid=(nb,),
            in_specs=[pl.BlockSpec((2 * bm,), lambda i, be, nu: (i // 2,), memory_space=pltpu.SMEM),
                      pl.BlockSpec((2 * bm,), lambda i, be, nu: (jnp.minimum(i + 1, nb - 1) // 2,),
                                   memory_space=pltpu.SMEM),
                      pl.BlockSpec(memory_space=pl.ANY),
                      pl.BlockSpec((1, D_MODEL, 2 * D_FF), wmap), pl.BlockSpec((1, 1, 2 * D_FF), wmap),
                      pl.BlockSpec((1, D_FF, D_MODEL), wmap), pl.BlockSpec((1, 1, D_MODEL), wmap)],
            out_specs=pl.BlockSpec((bm, D_MODEL), lambda i, be, nu: (i, 0)),
            scratch_shapes=[pltpu.VMEM((2, bm, D_MODEL), xn.dtype), pltpu.SemaphoreType.DMA((2,))]),
        out_shape=jax.ShapeDtypeStruct((p, D_MODEL), F32),
        compiler_params=_cparams(("arbitrary",)),
        name="experts",
    )(blk_exp, n_used, slot_tok, slot_tok, xn, w_up, b_up.reshape(N_EXPERTS, 1, 2 * D_FF), w_down,
      b_down.reshape(N_EXPERTS, 1, D_MODEL))


def _final_kernel(idx_ref, idxn_ref, h_ref, yb_ref, tg_ref, nf_ref, o_ref, ybuf_ref, sem_ref):
    i = pl.program_id(0)
    slot = i & 1
    tm = h_ref.shape[0]
    rows = _GatheredRows(yb_ref, ybuf_ref, sem_ref)

    @pl.when(i == 0)
    def _():
        rows.start(idx_ref, 0, slot)

    @pl.when(i + 1 < pl.num_programs(0))
    def _():
        rows.start(idxn_ref, 0, 1 - slot)

    rows.wait(slot)
    tg = tg_ref[...]
    f = tg[:, 0:1] * ybuf_ref[slot, pl.ds(0, tm), :]
    for kx in range(1, TOP_K):
        f = f + tg[:, kx:kx + 1] * ybuf_ref[slot, pl.ds(kx * tm, tm), :]
    y = h_ref[...] + f
    ms = jnp.mean(y * y, axis=-1, keepdims=True)
    o_ref[...] = y * lax.rsqrt(ms + RMS_EPS) * nf_ref[...]


def _final(h, yb, slot_of, tg, norm_f):
    t = h.shape[0]
    tm = TOK_TILE
    nt = t // tm
    idx = slot_of.reshape(nt, tm, TOP_K).transpose(0, 2, 1).reshape(nt * TOP_K * tm)
    row = lambda i: (i, 0)
    return pl.pallas_call(
        _final_kernel,
        grid=(nt,),
        in_specs=[pl.BlockSpec((TOP_K * tm,), lambda i: (i,), memory_space=pltpu.SMEM),
                  pl.BlockSpec((TOP_K * tm,), lambda i: (jnp.minimum(i + 1, nt - 1),), memory_space=pltpu.SMEM),
                  pl.BlockSpec((tm, D_MODEL), row), pl.BlockSpec(memory_space=pl.ANY),
                  pl.BlockSpec((tm, LANES), row), pl.BlockSpec((1, D_MODEL), lambda i: (0, 0))],
        out_specs=pl.BlockSpec((tm, D_MODEL), row),
        out_shape=jax.ShapeDtypeStruct((t, D_MODEL), F32),
        scratch_shapes=[pltpu.VMEM((2, TOP_K * tm, D_MODEL), yb.dtype), pltpu.SemaphoreType.DMA((2,))],
        compiler_params=_cparams(("arbitrary",)),
        name="final",
    )(idx, idx, h, yb, tg, norm_f.reshape(1, D_MODEL))


def _routing(top_e, bm):
    t = top_e.shape[0]
    a = t * TOP_K
    e_flat = top_e.reshape(a)
    order = jnp.argsort(e_flat).astype(jnp.int32)
    e_sorted = e_flat[order]
    counts = jnp.bincount(e_flat, length=N_EXPERTS).astype(jnp.int32)
    starts = jnp.cumsum(counts) - counts
    padded = (counts + bm - 1) // bm * bm
    pends = jnp.cumsum(padded)
    pstarts = pends - padded
    slot_sorted = (pstarts[e_sorted] + jnp.arange(a, dtype=jnp.int32) - starts[e_sorted]).astype(jnp.int32)
    n_blocks = _round_up(-(-a // bm) + N_EXPERTS, IDX_BLOCKS)
    p = n_blocks * bm
    blk_exp = jnp.minimum(jnp.sum(jnp.arange(n_blocks)[:, None] * bm >= pends[None, :], axis=1),
                          N_EXPERTS - 1).astype(jnp.int32)
    slot = jnp.arange(p, dtype=jnp.int32)
    e_slot = jnp.repeat(blk_exp, bm)
    j = slot - pstarts[e_slot]
    src = jnp.clip(starts[e_slot] + j, 0, a - 1)
    slot_tok = jnp.where((j < counts[e_slot]) & (slot < pends[-1]), order[src] // TOP_K, 0).astype(jnp.int32)
    slot_of = slot_sorted[jnp.argsort(order)].reshape(t, TOP_K)
    n_used = (pends[-1:] // bm).astype(jnp.int32)
    return slot_tok, slot_of, blk_exp, n_used


def kernel(x_prompt, x_sample, cache_kv, cache_win, state_wkv, state_shift, page_table, norm1, w_in, cmp_pe, cmp_mix, cmp_w1, cmp_w2, rw_mu, rw_w0, rw_w2, rw_a0, rw_a2, rw_g2, rw_kk, rw_ka, rw_rk, rw_ln_w, rw_ln_b, w_pa, w_pb, w_o, norm2, w_router, b_router, w_up, b_up, w_down, b_down, norm_f):
    assert w_in.shape[0] == 1, "single-layer trunk"
    b, s, _ = x_prompt.shape
    db, ds, _ = x_sample.shape
    n_pages = page_table.shape[1]
    assert cache_kv.shape[2] == PAGE
    past = n_pages * PAGE
    nbuf = cache_win.shape[2]
    tp, ts = b * s, db * ds
    t_real = tp + ts
    t_pad = _round_up(t_real, TOK_ALIGN)

    wi = w_in[0]
    w_pad = jnp.concatenate(
        [wi[:, :OFF_GN], jnp.pad(wi[:, OFF_GN:OFF_RW], ((0, 0), (0, GN_PAD - 3 * NSA_HEADS))), wi[:, OFF_RW:]],
        axis=1).astype(BF16)
    tile_g = lambda p: jnp.tile(p, (1, 1, NSA_KV_HEADS))
    mix2, pe2 = tile_g(cmp_mix[0].astype(F32)), tile_g(cmp_pe[0].astype(F32))
    bd = lambda w: jnp.stack([jnp.kron(jnp.eye(NSA_KV_HEADS, dtype=F32), w[i].astype(F32)) for i in range(2)])
    w1bd, w2bd = bd(cmp_w1[0]), bd(cmp_w2[0])
    zl = jnp.zeros((RW_DECAY_LORA, RW_WIDTH), F32)
    wlora = jnp.concatenate([jnp.concatenate([rw_w2[0], zl], axis=1),
                             jnp.concatenate([zl, rw_a2[0]], axis=1)], axis=0).astype(BF16)
    vec = lambda p: p[0].reshape(1, -1).astype(F32)
    wr_pad = jnp.pad(w_router[0].astype(F32), ((0, 0), (0, LANES - N_EXPERTS)))
    br_pad = jnp.concatenate([b_router[0].astype(F32), jnp.full((LANES - N_EXPERTS,), NEG_INF, F32)]).reshape(1, LANES)

    x_all = jnp.concatenate([x_prompt.reshape(tp, D_MODEL), x_sample.reshape(ts, D_MODEL)], axis=0)
    x_all = jnp.pad(x_all, ((0, t_pad - t_real), (0, 0)))
    q_all, kv_all, kvb_all, gn_all, rw_all, mg_all = _proj(x_all, norm1[0], w_pad)

    kv_p = kv_all[:tp].reshape(b, s, 6 * KV_LANES)
    kv_s = kv_all[tp:t_real].reshape(db, ds, 6 * KV_LANES)
    kvb_p = kvb_all[:tp].reshape(b, s, 6 * KV_LANES)

    tq = NSA_TQ
    assert s % tq == 0 and ds <= PAGE
    kc_p, vc_p = _compress(kv_p, mix2, pe2, w1bd, w2bd)
    o_nsa_p = _nsa(_block_diag_qt(q_all[:tp], b, s, tq), _gate_cols(gn_all[:tp], b, s, tq), kc_p, vc_p,
                   kvb_p, 2, kvb_p, 4, tq=tq, q_start=0, kw_start=0, ns_true=s // SEL_BLOCK)

    row_w = 4 * KV_LANES
    tail = jnp.pad(kv_s[:, :, :row_w], ((0, 0), (0, PAGE - ds), (0, 0)))
    cache2d = cache_kv[0].reshape(-1, PAGE, row_w)
    kc_s, vc_s = _compress_paged(page_table, cache2d, tail, mix2, pe2, w1bd, w2bd)
    kw_full = jnp.concatenate([cache_win[0].reshape(db, nbuf, 2 * KV_LANES), kv_s[:, :, row_w:]], axis=1)
    lw = _round_up(nbuf + ds, KEY_TILE)
    kw_pad = jnp.pad(kw_full, ((0, 0), (0, lw - (nbuf + ds)), (0, 0)))
    lp = _round_up(past + ds, SEL_BLOCK)
    tqs = NSA_TQ_FEW if ds <= NSA_TQ_FEW else NSA_TQ
    assert ds <= tqs
    pad_q = lambda x: jnp.pad(x.reshape(db, ds, -1), ((0, 0), (0, tqs - ds), (0, 0))).reshape(db * tqs, -1)
    o_nsa_s = _nsa_paged(_block_diag_qt(pad_q(q_all[tp:t_real]), db, tqs, tqs),
                         _gate_cols(pad_q(gn_all[tp:t_real]), db, tqs, tqs), kc_s, vc_s, page_table, cache2d, tail,
                         kw_pad, tq=tqs, q_start=past, kw_start=past - nbuf, ns_true=lp // SEL_BLOCK)
    o_nsa = jnp.concatenate([_own_head_rows(o_nsa_p, b, s, tq),
                             _own_head_rows(o_nsa_s, db, tqs, tqs).reshape(db, tqs, NSA_WIDTH)[:, :ds].reshape(ts, NSA_WIDTH),
                             jnp.zeros((t_pad - t_real, NSA_WIDTH), BF16)], axis=0)

    rw_p = rw_all[:tp].reshape(b, s, RW_COLS)
    rw_s = rw_all[tp:t_real].reshape(db, ds, RW_COLS)
    prev_p = jnp.concatenate([jnp.zeros((b, 1, RW_COLS), F32), rw_p[:, :-1]], axis=1)
    prev_s = jnp.concatenate([state_shift[0][:, None].astype(F32), rw_s[:, :-1]], axis=1)
    prev_all = jnp.concatenate([prev_p.reshape(tp, RW_COLS), prev_s.reshape(ts, RW_COLS),
                                jnp.zeros((t_pad - t_real, RW_COLS), F32)], axis=0)
    assert tp % TOK_ALIGN == 0
    pre_args = (vec(rw_mu), vec(rw_w0), vec(rw_a0), vec(rw_kk), vec(rw_ka), wlora, rw_g2[0].astype(BF16))
    pre_p = _rwkv_pre(rw_all, prev_all, 0, tp, *pre_args)
    pre_s = _rwkv_pre(rw_all, prev_all, tp, _round_up(ts, TOK_TILE), *pre_args)
    o_rw_p, wkv_p = _rwkv_group(pre_p, b, s, jnp.zeros((b, RW_HEADS, RW_HEAD_DIM, RW_HEAD_DIM), F32),
                                rw_ln_w[0], rw_ln_b[0], rw_rk[0])
    o_rw_s, wkv_s = _rwkv_group([x[:ts] for x in pre_s], db, ds, state_wkv[0],
                                rw_ln_w[0], rw_ln_b[0], rw_rk[0])
    o_rw = jnp.concatenate([o_rw_p, o_rw_s, jnp.zeros((t_pad - t_real, RW_WIDTH), F32)], axis=0)

    h, xn2, te, tg = _merge(x_all, o_nsa, o_rw, mg_all, w_pa[0].astype(BF16), w_pb[0].astype(BF16),
                            w_o[0].astype(BF16), norm2[0], wr_pad, br_pad)

    slot_tok, slot_of, blk_exp, n_used = _routing(te[:, :TOP_K], MOE_BM)
    yb = _experts(blk_exp, n_used, slot_tok, xn2, w_up[0].astype(BF16), b_up[0].astype(F32), w_down[0].astype(BF16),
                  b_down[0].astype(F32))
    y_all = _final(h, yb, slot_of, tg, norm_f)

    y_prompt = y_all[:tp].reshape(b, s, D_MODEL)
    y_sample = y_all[tp:t_real].reshape(db, ds, D_MODEL)
    kv_shape = (4, NSA_KV_HEADS, HEAD_DIM)
    win_shape = (2, NSA_KV_HEADS, HEAD_DIM)
    kv_prompt = kv_p[:, :, :row_w].reshape((1, b, s) + kv_shape)
    kv_sample = kv_s[:, :, :row_w].reshape((1, db, ds) + kv_shape)
    keep_p = min(WINDOW, s)
    win_prompt = kv_p[:, s - keep_p:, row_w:].reshape((1, b, keep_p) + win_shape)
    win_sample = kw_full[:, nbuf + ds - nbuf:].reshape((1, db, nbuf) + win_shape)
    shift_prompt = rw_p[:, -1][None]
    shift_sample = rw_s[:, -1][None]
    return (y_prompt, y_sample, kv_prompt, kv_sample, win_prompt, win_sample,
            wkv_p[None], wkv_s[None], shift_prompt, shift_sample)
```
